```python
import jax, jax.numpy as jnp
from jax import lax
import numpy as np

D_MODEL = 2048
BATCH = 2
SEQ = 4096
DEPTH = 2
DEC_BATCH = 32
DEC_SEQ = 8
PAST_LEN = 16384
PAGE_SIZE = 128

N_A_LAYERS = DEPTH // 2
N_B_LAYERS = DEPTH - N_A_LAYERS
CONV_WIDTH = 31
HEAD_DIM = 64
N_HEADS = D_MODEL // HEAD_DIM
N_KV_HEADS = 8
ROT_DIM = HEAD_DIM // 4
ROPE_THETA = 500000.0
WINDOW = 128
BLOCK = WINDOW
D_FF = 4 * D_MODEL
EPS = 1e-6

kernel_name = 'yoco_conformer_conv_swa_sink_decoder_step'

F32 = jnp.float32


def _rms(x, g):
    xf = x.astype(F32)
    y = xf * lax.rsqrt(jnp.mean(xf * xf, axis=-1, keepdims=True) + EPS)
    return (y * g.astype(F32)).astype(x.dtype)


def _layernorm(x, g, b):
    xf = x.astype(F32)
    mu = jnp.mean(xf, axis=-1, keepdims=True)
    var = jnp.mean(jnp.square(xf - mu), axis=-1, keepdims=True)
    return ((xf - mu) * lax.rsqrt(var + EPS) * g.astype(F32) + b.astype(F32)).astype(x.dtype)


def _rope(x, pos):
    half = ROT_DIM // 2
    inv = ROPE_THETA ** (-jnp.arange(half, dtype=F32) / half)
    ang = pos.astype(F32)[:, None] * inv[None, :]
    cos = jnp.cos(ang)[None, :, None, :]
    sin = jnp.sin(ang)[None, :, None, :]
    xr = x[..., :ROT_DIM].astype(F32)
    x1, x2 = xr[..., :half], xr[..., half:]
    rot = jnp.concatenate([x1 * cos - x2 * sin, x2 * cos + x1 * sin], axis=-1).astype(x.dtype)
    return jnp.concatenate([rot, x[..., ROT_DIM:]], axis=-1)


def _conv_module(h, ctx, w_pw1, b_pw1, w_dw, b_dw, ln_g, ln_b, w_pw2, b_pw2):
    u = h @ w_pw1 + b_pw1
    a, gate = u[..., :D_MODEL], u[..., D_MODEL:]
    u = a * jax.nn.sigmoid(gate)
    full = jnp.concatenate([ctx.astype(u.dtype), u], axis=1)
    c = lax.conv_general_dilated(full, w_dw[:, None, :].astype(u.dtype), (1,), 'VALID',
                                 dimension_numbers=('NWC', 'WIO', 'NWC'),
                                 feature_group_count=D_MODEL) + b_dw
    c = jax.nn.silu(_layernorm(c, ln_g, ln_b))
    out = c @ w_pw2 + b_pw2
    return out, full[:, -(CONV_WIDTH - 1):]


def _block_attend(q, k, v, q_pos, k_pos, sinks):
    b, nb, bq, h, hd = q.shape
    kvh = k.shape[3]
    g = h // kvh
    qg = q.reshape(b, nb, bq, kvh, g, hd)
    s = jnp.einsum('bnqkgd,bnskd->bnkgqs', qg, k).astype(F32) * (hd ** -0.5)
    rel = q_pos[:, :, None] - k_pos[:, None, :]
    ok = (rel >= 0) & (rel < WINDOW) & (k_pos[:, None, :] >= 0)
    s = jnp.where(ok[None, :, None, None], s, -jnp.inf)
    sink = sinks.astype(F32).reshape(kvh, g)[None, None, :, :, None, None]
    m = jnp.maximum(jnp.max(s, axis=-1, keepdims=True), sink)
    p = jnp.exp(s - m)
    denom = jnp.sum(p, axis=-1, keepdims=True) + jnp.exp(sink - m)
    p = (p / denom).astype(v.dtype)
    o = jnp.einsum('bnkgqs,bnskd->bnqkgd', p, v)
    return o.reshape(b, nb, bq, h, hd)


def _swa_prompt(q, k, v, sinks):
    b, t, h, hd = q.shape
    kvh = k.shape[2]
    nb = t // BLOCK
    qb = q.reshape(b, nb, BLOCK, h, hd)
    kb = k.reshape(b, nb, BLOCK, kvh, hd)
    vb = v.reshape(b, nb, BLOCK, kvh, hd)
    pad = jnp.zeros_like(kb[:, :1])
    kk = jnp.concatenate([jnp.concatenate([pad, kb[:, :-1]], axis=1), kb], axis=2)
    vv = jnp.concatenate([jnp.concatenate([pad, vb[:, :-1]], axis=1), vb], axis=2)
    starts = jnp.arange(nb, dtype=jnp.int32) * BLOCK
    q_pos = starts[:, None] + jnp.arange(BLOCK, dtype=jnp.int32)[None, :]
    k_pos = starts[:, None] + jnp.arange(-BLOCK, BLOCK, dtype=jnp.int32)[None, :]
    return _block_attend(qb, kk, vv, q_pos, k_pos, sinks).reshape(b, t, h, hd)


def _swa_sample(q, k_new, v_new, k_buf, v_buf, sinks):
    s = q.shape[1]
    L = k_buf.shape[1]
    kk = jnp.concatenate([k_buf.astype(k_new.dtype), k_new], axis=1)[:, None]
    vv = jnp.concatenate([v_buf.astype(v_new.dtype), v_new], axis=1)[:, None]
    q_pos = (PAST_LEN + jnp.arange(s, dtype=jnp.int32))[None]
    k_pos = (PAST_LEN - L + jnp.arange(L + s, dtype=jnp.int32))[None]
    return _block_attend(q[:, None], kk, vv, q_pos, k_pos, sinks)[:, 0]


def _trunk(x, pos, conv_ctx, k_buf, v_buf, p):
    b, t, _ = x.shape
    conv_new = []
    k = v = None
    for l in range(DEPTH):
        if l < N_A_LAYERS:
            h = _rms(x, p['norm_mix'][l])
            out, ctx = _conv_module(h, conv_ctx[l], p['w_pw1'][l], p['b_pw1'][l], p['w_dw'][l],
                                    p['b_dw'][l], p['conv_ln_g'][l], p['conv_ln_b'][l],
                                    p['w_pw2'][l], p['b_pw2'][l])
            conv_new.append(ctx)
            x = x + out
        else:
            j = l - N_A_LAYERS
            if l == N_A_LAYERS:
                hk = _rms(x, p['kv_norm'])
                k = _rope((hk @ p['w_k']).reshape(b, t, N_KV_HEADS, HEAD_DIM), pos)
                v = (hk @ p['w_v']).reshape(b, t, N_KV_HEADS, HEAD_DIM)
            hq = _rms(x, p['norm_mix'][l])
            q = _rope((hq @ p['w_q'][j]).reshape(b, t, N_HEADS, HEAD_DIM), pos)
            if k_buf is None:
                o = _swa_prompt(q, k, v, p['sinks'][j])
            else:
                o = _swa_sample(q, k, v, k_buf, v_buf, p['sinks'][j])
            x = x + o.reshape(b, t, N_HEADS * HEAD_DIM) @ p['w_o'][j]
        h = _rms(x, p['norm_mlp'][l])
        x = x + jnp.square(jax.nn.relu(h @ p['w_up'][l])) @ p['w_down'][l]
    return _rms(x, p['final_norm']), jnp.stack(conv_new), k, v


def setup_inputs(seed: int = 0) -> dict:
    key = jax.random.key(seed)
    ks = jax.random.split(key, 32)
    n = lambda i, shape, s: jax.random.normal(ks[i], shape, F32) * s
    win_rows = min(WINDOW, PAST_LEN)
    return {
        'x_prompt': n(0, (BATCH, SEQ, D_MODEL), 1.0),
        'x_sample': n(1, (DEC_BATCH, DEC_SEQ, D_MODEL), 1.0),
        'state_conv': n(2, (N_A_LAYERS, DEC_BATCH, CONV_WIDTH - 1, D_MODEL), 0.5),
        'cache_k': n(3, (DEC_BATCH, win_rows, N_KV_HEADS, HEAD_DIM), 1.0),
        'cache_v': n(4, (DEC_BATCH, win_rows, N_KV_HEADS, HEAD_DIM), 1.0),
        'norm_mix': 1.0 + n(5, (DEPTH, D_MODEL), 0.02),
        'w_pw1': n(6, (N_A_LAYERS, D_MODEL, 2 * D_MODEL), D_MODEL ** -0.5),
        'b_pw1': n(7, (N_A_LAYERS, 2 * D_MODEL), 0.02),
        'w_dw': n(8, (N_A_LAYERS, CONV_WIDTH, D_MODEL), CONV_WIDTH ** -0.5),
        'b_dw': n(9, (N_A_LAYERS, D_MODEL), 0.02),
        'conv_ln_g': 1.0 + n(10, (N_A_LAYERS, D_MODEL), 0.02),
        'conv_ln_b': n(11, (N_A_LAYERS, D_MODEL), 0.02),
        'w_pw2': n(12, (N_A_LAYERS, D_MODEL, D_MODEL), D_MODEL ** -0.5),
        'b_pw2': n(13, (N_A_LAYERS, D_MODEL), 0.02),
        'kv_norm': 1.0 + n(14, (D_MODEL,), 0.02),
        'w_k': n(15, (D_MODEL, N_KV_HEADS * HEAD_DIM), D_MODEL ** -0.5),
        'w_v': n(16, (D_MODEL, N_KV_HEADS * HEAD_DIM), D_MODEL ** -0.5),
        'w_q': n(17, (N_B_LAYERS, D_MODEL, N_HEADS * HEAD_DIM), D_MODEL ** -0.5),
        'w_o': n(18, (N_B_LAYERS, N_HEADS * HEAD_DIM, D_MODEL), (N_HEADS * HEAD_DIM) ** -0.5),
        'sinks': n(19, (N_B_LAYERS, N_HEADS), 0.5),
        'norm_mlp': 1.0 + n(20, (DEPTH, D_MODEL), 0.02),
        'w_up': n(21, (DEPTH, D_MODEL, D_FF), D_MODEL ** -0.5),
        'w_down': n(22, (DEPTH, D_FF, D_MODEL), D_FF ** -0.5),
        'final_norm': 1.0 + n(23, (D_MODEL,), 0.02),
    }


def reference(x_prompt, x_sample, state_conv, cache_k, cache_v, norm_mix, w_pw1, b_pw1, w_dw, b_dw,
              conv_ln_g, conv_ln_b, w_pw2, b_pw2, kv_norm, w_k, w_v, w_q, w_o, sinks, norm_mlp,
              w_up, w_down, final_norm):
    p = dict(norm_mix=norm_mix, w_pw1=w_pw1, b_pw1=b_pw1, w_dw=w_dw, b_dw=b_dw,
             conv_ln_g=conv_ln_g, conv_ln_b=conv_ln_b, w_pw2=w_pw2, b_pw2=b_pw2,
             kv_norm=kv_norm, w_k=w_k, w_v=w_v, w_q=w_q, w_o=w_o, sinks=sinks,
             norm_mlp=norm_mlp, w_up=w_up, w_down=w_down, final_norm=final_norm)
    zero_ctx = jnp.zeros((N_A_LAYERS, x_prompt.shape[0], CONV_WIDTH - 1, D_MODEL), x_prompt.dtype)
    pos_p = jnp.arange(x_prompt.shape[1], dtype=jnp.int32)
    y_prompt, conv_prompt, k_p, v_p = _trunk(x_prompt, pos_p, zero_ctx, None, None, p)
    pos_s = PAST_LEN + jnp.arange(x_sample.shape[1], dtype=jnp.int32)
    y_sample, conv_sample, k_s, v_s = _trunk(x_sample, pos_s, state_conv, cache_k, cache_v, p)
    L = cache_k.shape[1]
    win_k_prompt = k_p[:, -WINDOW:]
    win_v_prompt = v_p[:, -WINDOW:]
    win_k_sample = jnp.concatenate([cache_k, k_s.astype(cache_k.dtype)], axis=1)[:, -L:]
    win_v_sample = jnp.concatenate([cache_v, v_s.astype(cache_v.dtype)], axis=1)[:, -L:]
    return (y_prompt, y_sample, conv_prompt, conv_sample, win_k_prompt, win_v_prompt, win_k_sample, win_v_sample)
```

```python
import functools

import jax
import jax.numpy as jnp
from jax import lax
from jax.experimental import pallas as pl
from jax.experimental.pallas import tpu as pltpu

F32 = jnp.float32
BF16 = jnp.bfloat16

EPS = 1e-6
HEAD_DIM = 64
N_KV_HEADS = 8
ROT_DIM = HEAD_DIM // 4
ROPE_THETA = 500000.0
WINDOW = 128
CONV_WIDTH = 31
PAST_LEN = 16384

LANES = 128
SUBLANES = 8
VMEM_LIMIT_BYTES = 54 * 1024 * 1024

HALO = -(-(CONV_WIDTH - 1) // SUBLANES) * SUBLANES
TAP0 = HALO - (CONV_WIDTH - 1)


def _params(*semantics):
    return pltpu.CompilerParams(dimension_semantics=semantics, vmem_limit_bytes=VMEM_LIMIT_BYTES)


def _resident(shape):
    return pl.BlockSpec(shape, lambda *_: (0,) * len(shape), pipeline_mode=pl.Buffered(1))


def _rms_scale(x):
    return lax.rsqrt(jnp.mean(x * x, axis=-1, keepdims=True) + EPS)


def _glu_kernel(x_ref, g_ref, wa_ref, wg_ref, ba_ref, bg_ref, u_ref, h_ref):
    @pl.when(pl.program_id(1) == 0)
    def _():
        x = x_ref[...]
        h_ref[...] = (x * _rms_scale(x) * g_ref[...]).astype(BF16)

    h = h_ref[...]
    a = jnp.dot(h, wa_ref[...], preferred_element_type=F32) + ba_ref[...]
    gate = jnp.dot(h, wg_ref[...], preferred_element_type=F32) + bg_ref[...]
    u_ref[...] = a * jax.nn.sigmoid(gate)


def _glu(x, g, w, b, *, tm, tn):
    m, d = x.shape
    nj = d // tn
    return pl.pallas_call(
        _glu_kernel,
        grid=(m // tm, nj),
        in_specs=[
            pl.BlockSpec((tm, d), lambda i, j: (i, 0)),
            pl.BlockSpec((1, d), lambda i, j: (0, 0)),
            pl.BlockSpec((d, tn), lambda i, j: (0, j)),
            pl.BlockSpec((d, tn), lambda i, j: (0, j + nj)),
            pl.BlockSpec((1, tn), lambda i, j: (0, j)),
            pl.BlockSpec((1, tn), lambda i, j: (0, j + nj)),
        ],
        out_specs=pl.BlockSpec((tm, tn), lambda i, j: (i, j)),
        out_shape=jax.ShapeDtypeStruct((m, d), F32),
        scratch_shapes=[pltpu.VMEM((tm, d), BF16)],
        compiler_params=_params("parallel", "arbitrary"),
        name="glu",
    )(x, g, w, w, b, b)


def _ln_silu_pw2(c, x, lng_ref, lnb_ref, w2_ref, b2_ref):
    mu = jnp.mean(c, axis=-1, keepdims=True)
    cc = c - mu
    var = jnp.mean(cc * cc, axis=-1, keepdims=True)
    y = cc * lax.rsqrt(var + EPS) * lng_ref[...] + lnb_ref[...]
    y = y * jax.nn.sigmoid(y)
    return x + jnp.dot(y.astype(BF16), w2_ref[...], preferred_element_type=F32) + b2_ref[...]


CONV_ROWS = 64
CONV_LANES = 128


def _conv_prompt_kernel(u_ref, halo_ref, x_ref, wdw_ref, bdw_ref, lng_ref, lnb_ref, w2_ref, b2_ref,
                        o_ref, full_ref, c_ref):
    tt, d = c_ref.shape
    keep = (pl.program_id(1) > 0).astype(F32)
    full_ref[0:HALO, :] = halo_ref[0] * keep
    full_ref[HALO:, :] = u_ref[0]

    def chunk(ci, carry):
        r0 = pl.multiple_of(ci * CONV_ROWS, CONV_ROWS)
        for lc in range(d // CONV_LANES):
            lanes = slice(lc * CONV_LANES, (lc + 1) * CONV_LANES)
            win = full_ref[pl.ds(r0, CONV_ROWS + HALO), lanes]
            acc = jnp.zeros((CONV_ROWS, CONV_LANES), F32)
            for res in range(SUBLANES):
                taps = [k for k in range(CONV_WIDTH) if (TAP0 + k) % SUBLANES == res]
                top = max((TAP0 + k) // SUBLANES for k in taps) * SUBLANES
                shifted = win[res:res + top + CONV_ROWS]
                for k in taps:
                    a = (TAP0 + k) // SUBLANES * SUBLANES
                    acc = acc + shifted[a:a + CONV_ROWS] * wdw_ref[k:k + 1, lanes]
            c_ref[pl.ds(r0, CONV_ROWS), lanes] = acc + bdw_ref[:, lanes]
        return carry

    lax.fori_loop(0, tt // CONV_ROWS, chunk, 0)
    o_ref[0] = _ln_silu_pw2(c_ref[...], x_ref[0], lng_ref, lnb_ref, w2_ref, b2_ref)


def _conv_prompt(u, x, wdw, bdw, lng, lnb, w2, b2, *, tt):
    b, t, d = u.shape
    hb = tt // HALO
    return pl.pallas_call(
        _conv_prompt_kernel,
        grid=(b, t // tt),
        in_specs=[
            pl.BlockSpec((1, tt, d), lambda bi, ti: (bi, ti, 0)),
            pl.BlockSpec((1, HALO, d), lambda bi, ti: (bi, jnp.maximum(ti * hb - 1, 0), 0)),
            pl.BlockSpec((1, tt, d), lambda bi, ti: (bi, ti, 0)),
            _resident((CONV_WIDTH, d)),
            _resident((1, d)),
            _resident((1, d)),
            _resident((1, d)),
            _resident((d, d)),
            _resident((1, d)),
        ],
        out_specs=pl.BlockSpec((1, tt, d), lambda bi, ti: (bi, ti, 0)),
        out_shape=jax.ShapeDtypeStruct((b, t, d), F32),
        scratch_shapes=[pltpu.VMEM((HALO + tt, d), F32), pltpu.VMEM((tt, d), F32)],
        compiler_params=_params("parallel", "arbitrary"),
        name="conv_prompt",
    )(u, u, x, wdw, bdw, lng, lnb, w2, b2)


def _conv_sample_kernel(full_ref, x_ref, wdw_ref, bdw_ref, lng_ref, lnb_ref, w2_ref, b2_ref,
                        o_ref, c_ref):
    nb, rows, d = full_ref.shape
    s = rows - (CONV_WIDTH - 1)
    for lc in range(d // CONV_LANES):
        lanes = slice(lc * CONV_LANES, (lc + 1) * CONV_LANES)
        acc = jnp.zeros((nb, s, CONV_LANES), F32)
        for k in range(CONV_WIDTH):
            acc = acc + full_ref[:, k:k + s, lanes] * wdw_ref[k:k + 1, lanes]
        c_ref[:, lanes] = acc.reshape(nb * s, CONV_LANES) + bdw_ref[:, lanes]
    o_ref[...] = _ln_silu_pw2(c_ref[...], x_ref[...], lng_ref, lnb_ref, w2_ref, b2_ref)


def _conv_sample(full, x, wdw, bdw, lng, lnb, w2, b2):
    m, d = x.shape
    whole = lambda a: pl.BlockSpec(a.shape, lambda i: (0,) * a.ndim)
    args = (full, x, wdw, bdw, lng, lnb, w2, b2)
    return pl.pallas_call(
        _conv_sample_kernel,
        grid=(1,),
        in_specs=[whole(a) for a in args],
        out_specs=pl.BlockSpec((m, d), lambda i: (0, 0)),
        out_shape=jax.ShapeDtypeStruct((m, d), F32),
        scratch_shapes=[pltpu.VMEM((m, d), F32)],
        compiler_params=_params("arbitrary"),
        name="conv_sample",
    )(*args)


def _mlp_kernel(x_ref, g_ref, wu_ref, wd_ref, fg_ref, o_ref, h_ref, *, final):
    f = pl.program_id(1)

    @pl.when(f == 0)
    def _():
        x = x_ref[...]
        h_ref[...] = (x * _rms_scale(x) * g_ref[...]).astype(BF16)
        o_ref[...] = x

    hid = jnp.dot(h_ref[...], wu_ref[...], preferred_element_type=F32)
    hid = jnp.square(jnp.maximum(hid, 0.0)).astype(BF16)
    o_ref[...] += jnp.dot(hid, wd_ref[...], preferred_element_type=F32)

    if final:
        @pl.when(f == pl.num_programs(1) - 1)
        def _():
            y = o_ref[...]
            o_ref[...] = y * _rms_scale(y) * fg_ref[...]


def _mlp(x, g, wu, wd, fg, *, final, tm, tf):
    m, d = x.shape
    dff = wu.shape[1]
    return pl.pallas_call(
        functools.partial(_mlp_kernel, final=final),
        grid=(m // tm, dff // tf),
        in_specs=[
            pl.BlockSpec((tm, d), lambda i, f: (i, 0)),
            pl.BlockSpec((1, d), lambda i, f: (0, 0)),
            pl.BlockSpec((d, tf), lambda i, f: (0, f)),
            pl.BlockSpec((tf, d), lambda i, f: (f, 0)),
            pl.BlockSpec((1, d), lambda i, f: (0, 0)),
        ],
        out_specs=pl.BlockSpec((tm, d), lambda i, f: (i, 0)),
        out_shape=jax.ShapeDtypeStruct((m, d), F32),
        scratch_shapes=[pltpu.VMEM((tm, d), BF16)],
        compiler_params=_params("parallel", "arbitrary"),
        name="mlp_final" if final else "mlp",
    )(x, g, wu, wd, fg)


def _rope_tables(pos):
    half = ROT_DIM // 2
    inv = ROPE_THETA ** (-jnp.arange(half, dtype=F32) / half)
    ang = pos.astype(F32)[:, None] * inv[None, :]
    cos, sin = jnp.cos(ang), jnp.sin(ang)
    ones = jnp.ones((pos.shape[0], HEAD_DIM - ROT_DIM), F32)
    zeros = jnp.zeros_like(ones)
    zh = jnp.zeros_like(sin)
    per_head = lambda parts: jnp.tile(jnp.concatenate(parts, axis=1), (1, LANES // HEAD_DIM))
    return per_head([cos, cos, ones]), per_head([zh, sin, zeros]), per_head([-sin, zh, zeros])


def _rope_block(blk, cos, sin_up, sin_dn):
    half = ROT_DIM // 2
    return (blk * cos + pltpu.roll(blk, half, 1) * sin_up
            + pltpu.roll(blk, LANES - half, 1) * sin_dn)


def _qkv_kernel(x_ref, gq_ref, gk_ref, w_ref, cos_ref, sup_ref, sdn_ref, q_ref, kv_ref,
                hq_ref, hk_ref, *, nq):
    j = pl.program_id(1)
    tn = w_ref.shape[1]
    k_width = kv_ref.shape[1] // 2

    @pl.when(j == 0)
    def _():
        x = x_ref[...]
        xn = x * _rms_scale(x)
        hq_ref[...] = (xn * gq_ref[...]).astype(BF16)
        hk_ref[...] = (xn * gk_ref[...]).astype(BF16)

    cos, sup, sdn = cos_ref[...], sup_ref[...], sdn_ref[...]

    @pl.when(j < nq)
    def _():
        res = jnp.dot(hq_ref[...], w_ref[...], preferred_element_type=F32)
        for c in range(tn // LANES):
            lanes = slice(c * LANES, (c + 1) * LANES)
            q_ref[:, lanes] = _rope_block(res[:, lanes], cos, sup, sdn)

    @pl.when(j == nq)
    def _():
        res = jnp.dot(hk_ref[...], w_ref[...], preferred_element_type=F32)
        for c in range(k_width // LANES):
            lanes = slice(c * LANES, (c + 1) * LANES)
            kv_ref[:, lanes] = _rope_block(res[:, lanes], cos, sup, sdn)
        kv_ref[:, k_width:] = res[:, k_width:]


def _qkv(x, gq, gk, w, tables, *, tm, table_blocks):
    m, d = x.shape
    n_kv = w.shape[1] - d
    tn = n_kv
    nq = d // tn
    tspec = pl.BlockSpec((tm, LANES), lambda i, j: (i % table_blocks, 0))
    return pl.pallas_call(
        functools.partial(_qkv_kernel, nq=nq),
        grid=(m // tm, nq + 1),
        in_specs=[
            pl.BlockSpec((tm, d), lambda i, j: (i, 0)),
            pl.BlockSpec((1, d), lambda i, j: (0, 0)),
            pl.BlockSpec((1, d), lambda i, j: (0, 0)),
            pl.BlockSpec((d, tn), lambda i, j: (0, j)),
            tspec, tspec, tspec,
        ],
        out_specs=[
            pl.BlockSpec((tm, tn), lambda i, j: (i, jnp.minimum(j, nq - 1))),
            pl.BlockSpec((tm, n_kv), lambda i, j: (i, 0)),
        ],
        out_shape=[jax.ShapeDtypeStruct((m, d), F32), jax.ShapeDtypeStruct((m, n_kv), F32)],
        scratch_shapes=[pltpu.VMEM((tm, d), BF16), pltpu.VMEM((tm, d), BF16)],
        compiler_params=_params("parallel", "arbitrary"),
        name="qkv",
    )(x, gq, gk, w, *tables)


def _softmax_pv(s, sink, vv):
    m = jnp.maximum(jnp.max(s, axis=-1, keepdims=True), sink)
    p = jnp.exp(s - m)
    denom = jnp.sum(p, axis=-1, keepdims=True) + jnp.exp(sink - m)
    return jnp.dot(p.astype(BF16), vv, preferred_element_type=F32) * (1.0 / denom)


def _qk(q, k):
    return lax.dot_general(q, k, (((1,), (1,)), ((), ())), preferred_element_type=F32)


def _attn_prompt_kernel(q_ref, kvp_ref, kvc_ref, x_ref, wo_ref, sink_ref, o_ref, ocat_ref):
    blk = q_ref.shape[1]
    k_width = kvc_ref.shape[2] // 2
    group = q_ref.shape[2] // k_width
    row = lax.broadcasted_iota(jnp.int32, (blk, 2 * blk), 0)
    col = lax.broadcasted_iota(jnp.int32, (blk, 2 * blk), 1)
    prev_floor = jnp.where(pl.program_id(1) > 0, row, blk)
    valid = ((col < blk) & (col > prev_floor)) | ((col >= blk) & (col - blk <= row))
    bias = jnp.where(valid, 0.0, -jnp.inf).astype(F32)
    scale = HEAD_DIM ** -0.5

    for j in range(k_width // HEAD_DIM):
        kl = slice(j * HEAD_DIM, (j + 1) * HEAD_DIM)
        vl = slice(k_width + j * HEAD_DIM, k_width + (j + 1) * HEAD_DIM)
        kk = jnp.concatenate([kvp_ref[0, :, kl], kvc_ref[0, :, kl]], axis=0).astype(BF16)
        vv = jnp.concatenate([kvp_ref[0, :, vl], kvc_ref[0, :, vl]], axis=0).astype(BF16)
        for g in range(group):
            h = j * group + g
            hl = slice(h * HEAD_DIM, (h + 1) * HEAD_DIM)
            qh = (q_ref[0, :, hl] * scale).astype(BF16)
            o = _softmax_pv(_qk(qh, kk) + bias, sink_ref[h], vv)
            ocat_ref[:, hl] = o.astype(BF16)

    o_ref[0] = x_ref[0] + jnp.dot(ocat_ref[...], wo_ref[...], preferred_element_type=F32)


def _attn_prompt(q, kv, x, wo, sinks):
    b, t, d = x.shape
    n_kv = kv.shape[2]
    blk = WINDOW
    return pl.pallas_call(
        _attn_prompt_kernel,
        grid=(b, t // blk),
        in_specs=[
            pl.BlockSpec((1, blk, d), lambda bi, n: (bi, n, 0)),
            pl.BlockSpec((1, blk, n_kv), lambda bi, n: (bi, jnp.maximum(n - 1, 0), 0)),
            pl.BlockSpec((1, blk, n_kv), lambda bi, n: (bi, n, 0)),
            pl.BlockSpec((1, blk, d), lambda bi, n: (bi, n, 0)),
            _resident((d, d)),
            pl.BlockSpec(memory_space=pltpu.SMEM),
        ],
        out_specs=pl.BlockSpec((1, blk, d), lambda bi, n: (bi, n, 0)),
        out_shape=jax.ShapeDtypeStruct((b, t, d), F32),
        scratch_shapes=[pltpu.VMEM((blk, d), BF16)],
        compiler_params=_params("parallel", "arbitrary"),
        name="attn_prompt",
    )(q, kv, kv, x, wo, sinks)


def _attn_sample_kernel(q_ref, kvn_ref, ck_ref, cv_ref, x_ref, wo_ref, sink_ref,
                        o_ref, wk_ref, wv_ref, ocat_ref):
    gb, s, d = q_ref.shape
    win, k_width = ck_ref.shape[1], ck_ref.shape[2]
    group = d // k_width
    step = pl.program_id(0)

    wk_ref[:, 0:win - s, :] = ck_ref[:, s:win, :]
    wk_ref[:, win - s:win, :] = kvn_ref[:, :, 0:k_width]
    wv_ref[:, 0:win - s, :] = cv_ref[:, s:win, :]
    wv_ref[:, win - s:win, :] = kvn_ref[:, :, k_width:]

    qi = lax.broadcasted_iota(jnp.int32, (group * s, win + s), 0) % s
    kj = lax.broadcasted_iota(jnp.int32, (group * s, win + s), 1)
    rel = qi + win - kj
    bias = jnp.where((rel >= 0) & (rel < WINDOW), 0.0, -jnp.inf).astype(F32)
    head_of_row = lax.broadcasted_iota(jnp.int32, (group * s, 1), 0) // s
    scale = HEAD_DIM ** -0.5

    def one_batch(bb, carry):
        row0 = pl.multiple_of((step * gb + bb) * s, s)
        qb = q_ref[bb]
        kn = kvn_ref[bb]
        ck = ck_ref[bb]
        cv = cv_ref[bb]
        for j in range(k_width // HEAD_DIM):
            kl = slice(j * HEAD_DIM, (j + 1) * HEAD_DIM)
            vl = slice(k_width + j * HEAD_DIM, k_width + (j + 1) * HEAD_DIM)
            kk = jnp.concatenate([ck[:, kl], kn[:, kl]], axis=0).astype(BF16)
            vv = jnp.concatenate([cv[:, kl], kn[:, vl]], axis=0).astype(BF16)
            heads = [j * group + g for g in range(group)]
            qs = jnp.concatenate([qb[:, h * HEAD_DIM:(h + 1) * HEAD_DIM] for h in heads], axis=0)
            sink = jnp.zeros((group * s, 1), F32)
            for g, h in enumerate(heads):
                sink = jnp.where(head_of_row == g, sink_ref[h], sink)
            o = _softmax_pv(_qk((qs * scale).astype(BF16), kk) + bias, sink, vv)
            for g, h in enumerate(heads):
                ocat_ref[pl.ds(row0, s), h * HEAD_DIM:(h + 1) * HEAD_DIM] = o[g * s:(g + 1) * s]
        return carry

    lax.fori_loop(0, gb, one_batch, 0)

    @pl.when(step == pl.num_programs(0) - 1)
    def _():
        o_ref[...] = x_ref[...] + jnp.dot(ocat_ref[...].astype(BF16), wo_ref[...],
                                          preferred_element_type=F32)


def _attn_sample(q, kvn, ck, cv, x, wo, sinks, *, gb):
    nb, s, d = q.shape
    win, k_width = ck.shape[1], ck.shape[2]
    m = nb * s
    cache_spec = pl.BlockSpec((gb, win, k_width), lambda i: (i, 0, 0))
    return pl.pallas_call(
        _attn_sample_kernel,
        grid=(nb // gb,),
        in_specs=[
            pl.BlockSpec((gb, s, d), lambda i: (i, 0, 0)),
            pl.BlockSpec((gb, s, 2 * k_width), lambda i: (i, 0, 0)),
            cache_spec,
            cache_spec,
            _resident((m, d)),
            _resident((d, d)),
            pl.BlockSpec(memory_space=pltpu.SMEM),
        ],
        out_specs=[pl.BlockSpec((m, d), lambda i: (0, 0)), cache_spec, cache_spec],
        out_shape=[
            jax.ShapeDtypeStruct((m, d), F32),
            jax.ShapeDtypeStruct(ck.shape, F32),
            jax.ShapeDtypeStruct(cv.shape, F32),
        ],
        scratch_shapes=[pltpu.VMEM((m, d), F32)],
        compiler_params=_params("arbitrary"),
        name="attn_sample",
    )(q, kvn, ck, cv, x, wo, sinks)


def _row(v):
    return v.reshape(1, -1)


def _tile(m, want):
    return want if m % want == 0 else m


def kernel(x_prompt, x_sample, state_conv, cache_k, cache_v, norm_mix, w_pw1, b_pw1, w_dw, b_dw,
           conv_ln_g, conv_ln_b, w_pw2, b_pw2, kv_norm, w_k, w_v, w_q, w_o, sinks, norm_mlp,
           w_up, w_down, final_norm):
    b, t, d = x_prompt.shape
    nb, s, _ = x_sample.shape
    win = cache_k.shape[1]
    k_width = cache_k.shape[2] * cache_k.shape[3]
    depth = norm_mlp.shape[0]
    n_conv = w_pw1.shape[0]
    assert depth == 2 and n_conv == 1 and w_q.shape[0] == 1, "one conv layer then one attention layer"
    assert t % WINDOW == 0 and win == WINDOW

    w_pw1b, w_pw2b = w_pw1.astype(BF16), w_pw2.astype(BF16)
    w_upb, w_downb = w_up.astype(BF16), w_down.astype(BF16)
    w_qkvb = jnp.concatenate([w_q[0], w_k, w_v], axis=1).astype(BF16)
    w_ob = w_o[0].astype(BF16)

    def tokens(x, *, conv, attend, tables, table_blocks):
        m = x.shape[0]
        tm = _tile(m, 512)
        u = _glu(x, _row(norm_mix[0]), w_pw1b[0], _row(b_pw1[0]), tm=tm, tn=512)
        x, conv_state = conv(u, x)
        x = _mlp(x, _row(norm_mlp[0]), w_upb[0], w_downb[0], _row(final_norm),
                 final=False, tm=tm, tf=512)
        q, kv = _qkv(x, _row(norm_mix[1]), _row(kv_norm), w_qkvb, tables, tm=tm,
                     table_blocks=table_blocks)
        x, extra = attend(q, kv, x)
        y = _mlp(x, _row(norm_mlp[1]), w_upb[1], w_downb[1], _row(final_norm),
                 final=True, tm=tm, tf=512)
        return y, conv_state, kv, extra

    conv_w = (w_dw[0], _row(b_dw[0]), _row(conv_ln_g[0]), _row(conv_ln_b[0]), w_pw2b[0],
              _row(b_pw2[0]))

    def conv_p(u, x):
        u3 = u.reshape(b, t, d)
        x3 = _conv_prompt(u3, x.reshape(b, t, d), *conv_w, tt=256)
        return x3.reshape(b * t, d), u3[:, t - (CONV_WIDTH - 1):]

    def attend_p(q, kv, x):
        x3 = _attn_prompt(q.reshape(b, t, d), kv.reshape(b, t, -1), x.reshape(b, t, d), w_ob,
                          sinks[0])
        return x3.reshape(b * t, d), None

    tm_p = _tile(b * t, 512)
    y_p, conv_p_state, kv_p, _ = tokens(
        x_prompt.reshape(b * t, d), conv=conv_p, attend=attend_p,
        tables=_rope_tables(jnp.arange(t, dtype=jnp.int32)), table_blocks=t // tm_p)
    kv_p = kv_p.reshape(b, t, 2, N_KV_HEADS, HEAD_DIM)[:, t - WINDOW:]

    pos_s = PAST_LEN + jnp.arange(s, dtype=jnp.int32)

    def conv_s(u, x):
        full = jnp.concatenate([state_conv[0], u.reshape(nb, s, d)], axis=1)
        return _conv_sample(full, x, *conv_w), full[:, -(CONV_WIDTH - 1):]

    def attend_s(q, kv, x):
        x, wk, wv = _attn_sample(q.reshape(nb, s, d), kv.reshape(nb, s, -1),
                                 cache_k.reshape(nb, win, k_width),
                                 cache_v.reshape(nb, win, k_width), x, w_ob, sinks[0], gb=8)
        return x, (wk.reshape(cache_k.shape), wv.reshape(cache_v.shape))

    tables_s = tuple(jnp.tile(tb, (nb, 1)) for tb in _rope_tables(pos_s))
    y_s, conv_s_state, _, (win_k_s, win_v_s) = tokens(
        x_sample.reshape(nb * s, d), conv=conv_s, attend=attend_s, tables=tables_s,
        table_blocks=1)

    return (y_p.reshape(b, t, d), y_s.reshape(nb, s, d), conv_p_state[None], conv_s_state[None],
            kv_p[:, :, 0], kv_p[:, :, 1], win_k_s, win_v_s)
```

```python
import functools

import jax
import jax.numpy as jnp
from jax import lax
from jax.experimental import pallas as pl
from jax.experimental.pallas import tpu as pltpu

F32 = jnp.float32
BF16 = jnp.bfloat16

EPS = 1e-6
HEAD_DIM = 64
N_KV_HEADS = 8
ROT_DIM = HEAD_DIM // 4
ROPE_THETA = 500000.0
WINDOW = 128
CONV_WIDTH = 31
PAST_LEN = 16384

LANES = 128
SUBLANES = 8
VMEM_LIMIT_BYTES = 54 * 1024 * 1024

HALO = -(-(CONV_WIDTH - 1) // SUBLANES) * SUBLANES
TAP0 = HALO - (CONV_WIDTH - 1)


def _params(*semantics):
    return pltpu.CompilerParams(dimension_semantics=semantics, vmem_limit_bytes=VMEM_LIMIT_BYTES)


def _resident(shape):
    return pl.BlockSpec(shape, lambda *_: (0,) * len(shape), pipeline_mode=pl.Buffered(1))


def _rms_scale(x):
    return lax.rsqrt(jnp.mean(x * x, axis=-1, keepdims=True) + EPS)


def _glu_kernel(x_ref, g_ref, wa_ref, wg_ref, ba_ref, bg_ref, u_ref, h_ref):
    @pl.when(pl.program_id(1) == 0)
    def _():
        x = x_ref[...]
        h_ref[...] = (x * _rms_scale(x) * g_ref[...]).astype(BF16)

    h = h_ref[...]
    a = jnp.dot(h, wa_ref[...], preferred_element_type=F32) + ba_ref[...]
    gate = jnp.dot(h, wg_ref[...], preferred_element_type=F32) + bg_ref[...]
    u_ref[...] = a * jax.nn.sigmoid(gate)


def _glu(x, g, w, b, *, tm, tn):
    m, d = x.shape
    nj = d // tn
    return pl.pallas_call(
        _glu_kernel,
        grid=(m // tm, nj),
        in_specs=[
            pl.BlockSpec((tm, d), lambda i, j: (i, 0)),
            pl.BlockSpec((1, d), lambda i, j: (0, 0)),
            pl.BlockSpec((d, tn), lambda i, j: (0, j)),
            pl.BlockSpec((d, tn), lambda i, j: (0, j + nj)),
            pl.BlockSpec((1, tn), lambda i, j: (0, j)),
            pl.BlockSpec((1, tn), lambda i, j: (0, j + nj)),
        ],
        out_specs=pl.BlockSpec((tm, tn), lambda i, j: (i, j)),
        out_shape=jax.ShapeDtypeStruct((m, d), F32),
        scratch_shapes=[pltpu.VMEM((tm, d), BF16)],
        compiler_params=_params("parallel", "arbitrary"),
        name="glu",
    )(x, g, w, w, b, b)


def _ln_silu_pw2(c, x, lng_ref, lnb_ref, w2_ref, b2_ref):
    mu = jnp.mean(c, axis=-1, keepdims=True)
    cc = c - mu
    var = jnp.mean(cc * cc, axis=-1, keepdims=True)
    y = cc * lax.rsqrt(var + EPS) * lng_ref[...] + lnb_ref[...]
    y = y * jax.nn.sigmoid(y)
    return x + jnp.dot(y.astype(BF16), w2_ref[...], preferred_element_type=F32) + b2_ref[...]


CONV_ROWS = 64
CONV_LANES = 128


def _conv_prompt_kernel(u_ref, halo_ref, x_ref, wdw_ref, bdw_ref, lng_ref, lnb_ref, w2_ref, b2_ref,
                        o_ref, full_ref, c_ref):
    tt, d = c_ref.shape
    keep = (pl.program_id(1) > 0).astype(F32)
    full_ref[0:HALO, :] = halo_ref[0] * keep
    full_ref[HALO:, :] = u_ref[0]

    def chunk(ci, carry):
        r0 = pl.multiple_of(ci * CONV_ROWS, CONV_ROWS)
        for lc in range(d // CONV_LANES):
            lanes = slice(lc * CONV_LANES, (lc + 1) * CONV_LANES)
            win = full_ref[pl.ds(r0, CONV_ROWS + HALO), lanes]
            acc = jnp.zeros((CONV_ROWS, CONV_LANES), F32)
            for res in range(SUBLANES):
                taps = [k for k in range(CONV_WIDTH) if (TAP0 + k) % SUBLANES == res]
                shifted = win if res == 0 else pltpu.roll(win, CONV_ROWS + HALO - res, 0)
                for k in taps:
                    a = (TAP0 + k) // SUBLANES * SUBLANES
                    acc = acc + shifted[a:a + CONV_ROWS] * wdw_ref[k:k + 1, lanes]
            c_ref[pl.ds(r0, CONV_ROWS), lanes] = acc + bdw_ref[:, lanes]
        return carry

    lax.fori_loop(0, tt // CONV_ROWS, chunk, 0)
    o_ref[0] = _ln_silu_pw2(c_ref[...], x_ref[0], lng_ref, lnb_ref, w2_ref, b2_ref)


def _conv_prompt(u, x, wdw, bdw, lng, lnb, w2, b2, *, tt):
    b, t, d = u.shape
    hb = tt // HALO
    return pl.pallas_call(
        _conv_prompt_kernel,
        grid=(b, t // tt),
        in_specs=[
            pl.BlockSpec((1, tt, d), lambda bi, ti: (bi, ti, 0)),
            pl.BlockSpec((1, HALO, d), lambda bi, ti: (bi, jnp.maximum(ti * hb - 1, 0), 0)),
            pl.BlockSpec((1, tt, d), lambda bi, ti: (bi, ti, 0)),
            _resident((CONV_WIDTH, d)),
            _resident((1, d)),
            _resident((1, d)),
            _resident((1, d)),
            _resident((d, d)),
            _resident((1, d)),
        ],
        out_specs=pl.BlockSpec((1, tt, d), lambda bi, ti: (bi, ti, 0)),
        out_shape=jax.ShapeDtypeStruct((b, t, d), F32),
        scratch_shapes=[pltpu.VMEM((HALO + tt, d), F32), pltpu.VMEM((tt, d), F32)],
        compiler_params=_params("parallel", "arbitrary"),
        name="conv_prompt",
    )(u, u, x, wdw, bdw, lng, lnb, w2, b2)


def _conv_sample_kernel(full_ref, x_ref, wdw_ref, bdw_ref, lng_ref, lnb_ref, w2_ref, b2_ref,
                        o_ref, c_ref):
    nb, rows, d = full_ref.shape
    s = rows - (CONV_WIDTH - 1)
    for lc in range(d // CONV_LANES):
        lanes = slice(lc * CONV_LANES, (lc + 1) * CONV_LANES)
        acc = jnp.zeros((nb, s, CONV_LANES), F32)
        for k in range(CONV_WIDTH):
            acc = acc + full_ref[:, k:k + s, lanes] * wdw_ref[k:k + 1, lanes]
        c_ref[:, lanes] = acc.reshape(nb * s, CONV_LANES) + bdw_ref[:, lanes]
    o_ref[...] = _ln_silu_pw2(c_ref[...], x_ref[...], lng_ref, lnb_ref, w2_ref, b2_ref)


def _conv_sample(full, x, wdw, bdw, lng, lnb, w2, b2):
    m, d = x.shape
    whole = lambda a: pl.BlockSpec(a.shape, lambda i: (0,) * a.ndim)
    args = (full, x, wdw, bdw, lng, lnb, w2, b2)
    return pl.pallas_call(
        _conv_sample_kernel,
        grid=(1,),
        in_specs=[whole(a) for a in args],
        out_specs=pl.BlockSpec((m, d), lambda i: (0, 0)),
        out_shape=jax.ShapeDtypeStruct((m, d), F32),
        scratch_shapes=[pltpu.VMEM((m, d), F32)],
        compiler_params=_params("arbitrary"),
        name="conv_sample",
    )(*args)


def _mlp_kernel(x_ref, g_ref, wu_ref, wd_ref, fg_ref, o_ref, h_ref, *, final):
    f = pl.program_id(1)

    @pl.when(f == 0)
    def _():
        x = x_ref[...]
        h_ref[...] = (x * _rms_scale(x) * g_ref[...]).astype(BF16)
        o_ref[...] = x

    hid = jnp.dot(h_ref[...], wu_ref[...], preferred_element_type=F32)
    hid = jnp.square(jnp.maximum(hid, 0.0)).astype(BF16)
    o_ref[...] += jnp.dot(hid, wd_ref[...], preferred_element_type=F32)

    if final:
        @pl.when(f == pl.num_programs(1) - 1)
        def _():
            y = o_ref[...]
            o_ref[...] = y * _rms_scale(y) * fg_ref[...]


def _mlp(x, g, wu, wd, fg, *, final, tm, tf):
    m, d = x.shape
    dff = wu.shape[1]
    return pl.pallas_call(
        functools.partial(_mlp_kernel, final=final),
        grid=(m // tm, dff // tf),
        in_specs=[
            pl.BlockSpec((tm, d), lambda i, f: (i, 0)),
            pl.BlockSpec((1, d), lambda i, f: (0, 0)),
            pl.BlockSpec((d, tf), lambda i, f: (0, f)),
            pl.BlockSpec((tf, d), lambda i, f: (f, 0)),
            pl.BlockSpec((1, d), lambda i, f: (0, 0)),
        ],
        out_specs=pl.BlockSpec((tm, d), lambda i, f: (i, 0)),
        out_shape=jax.ShapeDtypeStruct((m, d), F32),
        scratch_shapes=[pltpu.VMEM((tm, d), BF16)],
        compiler_params=_params("parallel", "arbitrary"),
        name="mlp_final" if final else "mlp",
    )(x, g, wu, wd, fg)


def _rope_tables(pos):
    half = ROT_DIM // 2
    inv = ROPE_THETA ** (-jnp.arange(half, dtype=F32) / half)
    ang = pos.astype(F32)[:, None] * inv[None, :]
    cos, sin = lax.optimization_barrier((jnp.cos(ang), jnp.sin(ang)))
    ones = jnp.ones((pos.shape[0], HEAD_DIM - ROT_DIM), F32)
    zeros = jnp.zeros_like(ones)
    zh = jnp.zeros_like(sin)
    per_head = lambda parts: jnp.tile(jnp.concatenate(parts, axis=1), (1, LANES // HEAD_DIM))
    return per_head([cos, cos, ones]), per_head([zh, sin, zeros]), per_head([-sin, zh, zeros])


def _rope_block(blk, cos, sin_up, sin_dn):
    half = ROT_DIM // 2
    return (blk * cos + pltpu.roll(blk, half, 1) * sin_up
            + pltpu.roll(blk, LANES - half, 1) * sin_dn)


def _qkv_kernel(x_ref, gq_ref, gk_ref, w_ref, cos_ref, sup_ref, sdn_ref, q_ref, kv_ref,
                hq_ref, hk_ref, *, nq):
    j = pl.program_id(1)
    tn = w_ref.shape[1]
    k_width = kv_ref.shape[1] // 2

    @pl.when(j == 0)
    def _():
        x = x_ref[...]
        xn = x * _rms_scale(x)
        hq_ref[...] = (xn * gq_ref[...]).astype(BF16)
        hk_ref[...] = (xn * gk_ref[...]).astype(BF16)

    cos, sup, sdn = cos_ref[...], sup_ref[...], sdn_ref[...]

    @pl.when(j < nq)
    def _():
        res = jnp.dot(hq_ref[...], w_ref[...], preferred_element_type=F32)
        for c in range(tn // LANES):
            lanes = slice(c * LANES, (c + 1) * LANES)
            q_ref[:, lanes] = _rope_block(res[:, lanes], cos, sup, sdn)

    @pl.when(j == nq)
    def _():
        res = jnp.dot(hk_ref[...], w_ref[...], preferred_element_type=F32)
        for c in range(k_width // LANES):
            lanes = slice(c * LANES, (c + 1) * LANES)
            kv_ref[:, lanes] = _rope_block(res[:, lanes], cos, sup, sdn)
        kv_ref[:, k_width:] = res[:, k_width:]


def _qkv(x, gq, gk, w, tables, *, tm, table_blocks):
    m, d = x.shape
    n_kv = w.shape[1] - d
    tn = n_kv
    nq = d // tn
    tspec = pl.BlockSpec((tm, LANES), lambda i, j: (i % table_blocks, 0))
    return pl.pallas_call(
        functools.partial(_qkv_kernel, nq=nq),
        grid=(m // tm, nq + 1),
        in_specs=[
            pl.BlockSpec((tm, d), lambda i, j: (i, 0)),
            pl.BlockSpec((1, d), lambda i, j: (0, 0)),
            pl.BlockSpec((1, d), lambda i, j: (0, 0)),
            pl.BlockSpec((d, tn), lambda i, j: (0, j)),
            tspec, tspec, tspec,
        ],
        out_specs=[
            pl.BlockSpec((tm, tn), lambda i, j: (i, jnp.minimum(j, nq - 1))),
            pl.BlockSpec((tm, n_kv), lambda i, j: (i, 0)),
        ],
        out_shape=[jax.ShapeDtypeStruct((m, d), F32), jax.ShapeDtypeStruct((m, n_kv), F32)],
        scratch_shapes=[pltpu.VMEM((tm, d), BF16), pltpu.VMEM((tm, d), BF16)],
        compiler_params=_params("parallel", "arbitrary"),
        name="qkv",
    )(x, gq, gk, w, *tables)


def _softmax_pv(s, sink, vv):
    m = jnp.maximum(jnp.max(s, axis=-1, keepdims=True), sink)
    p = jnp.exp(s - m)
    denom = jnp.sum(p, axis=-1, keepdims=True) + jnp.exp(sink - m)
    return jnp.dot(p.astype(BF16), vv, preferred_element_type=F32) * (1.0 / denom)


def _qk(q, k):
    return lax.dot_general(q, k, (((1,), (1,)), ((), ())), preferred_element_type=F32)


def _attn_prompt_kernel(q_ref, kvp_ref, kvc_ref, x_ref, wo_ref, sink_ref, o_ref, kv_ref, ot_ref):
    blk = kvp_ref.shape[1]
    n_blk = q_ref.shape[1] // blk
    k_width = kvc_ref.shape[2] // 2
    group = q_ref.shape[2] // k_width
    n_kv_heads = k_width // HEAD_DIM
    kv_ref[0:blk, :] = kvp_ref[0].astype(BF16)
    kv_ref[blk:, :] = kvc_ref[0].astype(BF16)

    key = lax.broadcasted_iota(jnp.int32, (2 * blk, group * blk), 0)
    qry = lax.broadcasted_iota(jnp.int32, (2 * blk, group * blk), 1) % blk
    own = (key >= blk) & (key - blk <= qry)
    bias_inner = jnp.where(((key < blk) & (key > qry)) | own, 0.0, -jnp.inf).astype(F32)
    bias_first = jnp.where(pl.program_id(1) > 0, bias_inner, jnp.where(own, 0.0, -jnp.inf))
    head_of_lane = lax.broadcasted_iota(jnp.int32, (1, group * blk), 1) // blk
    scale = HEAD_DIM ** -0.5

    def q_block(qb, carry):
        row0 = pl.multiple_of(qb * blk, blk)
        bias = jnp.where(qb == 0, bias_first, bias_inner)

        def scores(j):
            kk = kv_ref[pl.ds(row0, 2 * blk), j * HEAD_DIM:(j + 1) * HEAD_DIM]
            qs = jnp.concatenate(
                [q_ref[0, pl.ds(row0, blk), h * HEAD_DIM:(h + 1) * HEAD_DIM]
                 for h in range(j * group, (j + 1) * group)], axis=0)
            return _qk(kk, (qs * scale).astype(BF16)) + bias

        st = scores(0)
        for j in range(n_kv_heads):
            st_next = scores(j + 1) if j + 1 < n_kv_heads else None
            vv = kv_ref[pl.ds(row0, 2 * blk), k_width + j * HEAD_DIM:k_width + (j + 1) * HEAD_DIM]
            sink = jnp.zeros((1, group * blk), F32)
            for g in range(group):
                sink = jnp.where(head_of_lane == g, sink_ref[j * group + g], sink)
            m = jnp.maximum(jnp.max(st, axis=0, keepdims=True), sink)
            p = jnp.exp(st - m)
            denom = jnp.sum(p, axis=0, keepdims=True) + jnp.exp(sink - m)
            ot = lax.dot_general(vv, p.astype(BF16), (((0,), (0,)), ((), ())),
                                 preferred_element_type=F32) * (1.0 / denom)
            for g in range(group):
                h = j * group + g
                ot_ref[h * HEAD_DIM:(h + 1) * HEAD_DIM, pl.ds(row0, blk)] = (
                    ot[:, g * blk:(g + 1) * blk].astype(BF16))
            st = st_next
        return carry

    lax.fori_loop(0, n_blk, q_block, 0)

    proj = lax.dot_general(ot_ref[...], wo_ref[...], (((0,), (0,)), ((), ())),
                           preferred_element_type=F32)
    o_ref[0] = x_ref[0] + proj


def _attn_prompt(q, kv, x, wo, sinks, *, tq):
    b, t, d = x.shape
    n_kv = kv.shape[2]
    blk = WINDOW
    per = tq // blk
    return pl.pallas_call(
        _attn_prompt_kernel,
        grid=(b, t // tq),
        in_specs=[
            pl.BlockSpec((1, tq, d), lambda bi, n: (bi, n, 0)),
            pl.BlockSpec((1, blk, n_kv), lambda bi, n: (bi, jnp.maximum(n * per - 1, 0), 0)),
            pl.BlockSpec((1, tq, n_kv), lambda bi, n: (bi, n, 0)),
            pl.BlockSpec((1, tq, d), lambda bi, n: (bi, n, 0)),
            _resident((d, d)),
            pl.BlockSpec(memory_space=pltpu.SMEM),
        ],
        out_specs=pl.BlockSpec((1, tq, d), lambda bi, n: (bi, n, 0)),
        out_shape=jax.ShapeDtypeStruct((b, t, d), F32),
        scratch_shapes=[pltpu.VMEM((blk + tq, n_kv), BF16), pltpu.VMEM((d, tq), BF16)],
        compiler_params=_params("parallel", "arbitrary"),
        name="attn_prompt",
    )(q, kv, kv, x, wo, sinks)


def _attn_sample_kernel(q_ref, kvn_ref, ck_ref, cv_ref, x_ref, wo_ref, sink_ref,
                        o_ref, wk_ref, wv_ref, ocat_ref):
    gb, s, d = q_ref.shape
    win, k_width = ck_ref.shape[1], ck_ref.shape[2]
    group = d // k_width
    step = pl.program_id(0)

    wk_ref[:, 0:win - s, :] = ck_ref[:, s:win, :]
    wk_ref[:, win - s:win, :] = kvn_ref[:, :, 0:k_width]
    wv_ref[:, 0:win - s, :] = cv_ref[:, s:win, :]
    wv_ref[:, win - s:win, :] = kvn_ref[:, :, k_width:]

    qi = lax.broadcasted_iota(jnp.int32, (group * s, win + s), 0) % s
    kj = lax.broadcasted_iota(jnp.int32, (group * s, win + s), 1)
    rel = qi + win - kj
    bias = jnp.where((rel >= 0) & (rel < WINDOW), 0.0, -jnp.inf).astype(F32)
    head_of_row = lax.broadcasted_iota(jnp.int32, (group * s, 1), 0) // s
    scale = HEAD_DIM ** -0.5

    def one_batch(bb, carry):
        row0 = pl.multiple_of((step * gb + bb) * s, s)
        qb = q_ref[bb]
        kn = kvn_ref[bb]
        ck = ck_ref[bb]
        cv = cv_ref[bb]
        for j in range(k_width // HEAD_DIM):
            kl = slice(j * HEAD_DIM, (j + 1) * HEAD_DIM)
            vl = slice(k_width + j * HEAD_DIM, k_width + (j + 1) * HEAD_DIM)
            kk = jnp.concatenate([ck[:, kl], kn[:, kl]], axis=0).astype(BF16)
            vv = jnp.concatenate([cv[:, kl], kn[:, vl]], axis=0).astype(BF16)
            heads = [j * group + g for g in range(group)]
            qs = jnp.concatenate([qb[:, h * HEAD_DIM:(h + 1) * HEAD_DIM] for h in heads], axis=0)
            sink = jnp.zeros((group * s, 1), F32)
            for g, h in enumerate(heads):
                sink = jnp.where(head_of_row == g, sink_ref[h], sink)
            o = _softmax_pv(_qk((qs * scale).astype(BF16), kk) + bias, sink, vv)
            for g, h in enumerate(heads):
                ocat_ref[pl.ds(row0, s), h * HEAD_DIM:(h + 1) * HEAD_DIM] = o[g * s:(g + 1) * s]
        return carry

    lax.fori_loop(0, gb, one_batch, 0)

    @pl.when(step == pl.num_programs(0) - 1)
    def _():
        o_ref[...] = x_ref[...] + jnp.dot(ocat_ref[...].astype(BF16), wo_ref[...],
                                          preferred_element_type=F32)


def _attn_sample(q, kvn, ck, cv, x, wo, sinks, *, gb):
    nb, s, d = q.shape
    win, k_width = ck.shape[1], ck.shape[2]
    m = nb * s
    cache_spec = pl.BlockSpec((gb, win, k_width), lambda i: (i, 0, 0))
    return pl.pallas_call(
        _attn_sample_kernel,
        grid=(nb // gb,),
        in_specs=[
            pl.BlockSpec((gb, s, d), lambda i: (i, 0, 0)),
            pl.BlockSpec((gb, s, 2 * k_width), lambda i: (i, 0, 0)),
            cache_spec,
            cache_spec,
            _resident((m, d)),
            _resident((d, d)),
            pl.BlockSpec(memory_space=pltpu.SMEM),
        ],
        out_specs=[pl.BlockSpec((m, d), lambda i: (0, 0)), cache_spec, cache_spec],
        out_shape=[
            jax.ShapeDtypeStruct((m, d), F32),
            jax.ShapeDtypeStruct(ck.shape, F32),
            jax.ShapeDtypeStruct(cv.shape, F32),
        ],
        scratch_shapes=[pltpu.VMEM((m, d), F32)],
        compiler_params=_params("arbitrary"),
        name="attn_sample",
    )(q, kvn, ck, cv, x, wo, sinks)


def _row(v):
    return v.reshape(1, -1)


def _tile(m, want):
    return want if m % want == 0 else m


def kernel(x_prompt, x_sample, state_conv, cache_k, cache_v, norm_mix, w_pw1, b_pw1, w_dw, b_dw,
           conv_ln_g, conv_ln_b, w_pw2, b_pw2, kv_norm, w_k, w_v, w_q, w_o, sinks, norm_mlp,
           w_up, w_down, final_norm):
    b, t, d = x_prompt.shape
    nb, s, _ = x_sample.shape
    win = cache_k.shape[1]
    k_width = cache_k.shape[2] * cache_k.shape[3]
    depth = norm_mlp.shape[0]
    n_conv = w_pw1.shape[0]
    assert depth == 2 and n_conv == 1 and w_q.shape[0] == 1, "one conv layer then one attention layer"
    assert t % WINDOW == 0 and win == WINDOW

    w_pw1b, w_pw2b = w_pw1[0].astype(BF16), w_pw2[0].astype(BF16)
    w_upb = [w_up[l].astype(BF16) for l in range(depth)]
    w_downb = [w_down[l].astype(BF16) for l in range(depth)]
    w_qkvb = jnp.concatenate([w_q[0], w_k, w_v], axis=1).astype(BF16)
    w_ob = w_o[0].astype(BF16)

    def tokens(x, *, conv, attend, tables, table_blocks, tm_qkv):
        m = x.shape[0]
        tm = _tile(m, 1024)
        u = _glu(x, _row(norm_mix[0]), w_pw1b, _row(b_pw1[0]), tm=tm, tn=512)
        x, conv_state = conv(u, x)
        x = _mlp(x, _row(norm_mlp[0]), w_upb[0], w_downb[0], _row(final_norm),
                 final=False, tm=tm, tf=512)
        q, kv = _qkv(x, _row(norm_mix[1]), _row(kv_norm), w_qkvb, tables, tm=tm_qkv,
                     table_blocks=table_blocks)
        x, extra = attend(q, kv, x)
        y = _mlp(x, _row(norm_mlp[1]), w_upb[1], w_downb[1], _row(final_norm),
                 final=True, tm=tm, tf=512)
        return y, conv_state, kv, extra

    conv_w = (w_dw[0], _row(b_dw[0]), _row(conv_ln_g[0]), _row(conv_ln_b[0]), w_pw2b,
              _row(b_pw2[0]))

    def conv_p(u, x):
        u3 = u.reshape(b, t, d)
        x3 = _conv_prompt(u3, x.reshape(b, t, d), *conv_w, tt=256)
        return x3.reshape(b * t, d), u3[:, t - (CONV_WIDTH - 1):]

    def attend_p(q, kv, x):
        x3 = _attn_prompt(q.reshape(b, t, d), kv.reshape(b, t, -1), x.reshape(b, t, d), w_ob,
                          sinks[0], tq=512)
        return x3.reshape(b * t, d), None

    tm_p = _tile(b * t, 512)
    y_p, conv_p_state, kv_p, _ = tokens(
        x_prompt.reshape(b * t, d), conv=conv_p, attend=attend_p,
        tables=_rope_tables(jnp.arange(t, dtype=jnp.int32)), table_blocks=t // tm_p, tm_qkv=tm_p)
    kv_p = kv_p.reshape(b, t, -1)[:, t - WINDOW:].reshape(b, WINDOW, 2, N_KV_HEADS, HEAD_DIM)

    pos_s = PAST_LEN + jnp.arange(s, dtype=jnp.int32)

    def conv_s(u, x):
        full = jnp.concatenate([state_conv[0], u.reshape(nb, s, d)], axis=1)
        return _conv_sample(full, x, *conv_w), full[:, -(CONV_WIDTH - 1):]

    def attend_s(q, kv, x):
        x, wk, wv = _attn_sample(q.reshape(nb, s, d), kv.reshape(nb, s, -1),
                                 cache_k.reshape(nb, win, k_width),
                                 cache_v.reshape(nb, win, k_width), x, w_ob, sinks[0], gb=8)
        return x, (wk.reshape(cache_k.shape), wv.reshape(cache_v.shape))

    tables_s = tuple(jnp.tile(tb, (nb, 1)) for tb in _rope_tables(pos_s))
    y_s, conv_s_state, _, (win_k_s, win_v_s) = tokens(
        x_sample.reshape(nb * s, d), conv=conv_s, attend=attend_s, tables=tables_s,
        table_blocks=1, tm_qkv=nb * s)

    return (y_p.reshape(b, t, d), y_s.reshape(nb, s, d), conv_p_state[None], conv_s_state[None],
            kv_p[:, :, 0], kv_p[:, :, 1], win_k_s, win_v_s)
```

```python
import functools

import jax
import jax.numpy as jnp
from jax import lax
from jax.experimental import pallas as pl
from jax.experimental.pallas import tpu as pltpu

F32 = jnp.float32
BF16 = jnp.bfloat16

EPS = 1e-6
HEAD_DIM = 64
N_KV_HEADS = 8
ROT_DIM = HEAD_DIM // 4
ROPE_THETA = 500000.0
WINDOW = 128
CONV_WIDTH = 31
PAST_LEN = 16384

LANES = 128
SUBLANES = 8
VMEM_LIMIT_BYTES = 54 * 1024 * 1024

HALO = -(-(CONV_WIDTH - 1) // SUBLANES) * SUBLANES
TAP0 = HALO - (CONV_WIDTH - 1)


def _params(*semantics):
    return pltpu.CompilerParams(dimension_semantics=semantics, vmem_limit_bytes=VMEM_LIMIT_BYTES)


def _resident(shape):
    return pl.BlockSpec(shape, lambda *_: (0,) * len(shape), pipeline_mode=pl.Buffered(1))


def _rms_scale(x):
    return lax.rsqrt(jnp.mean(x * x, axis=-1, keepdims=True) + EPS)


def _glu_kernel(x_ref, g_ref, wa_ref, wg_ref, ba_ref, bg_ref, u_ref, h_ref):
    @pl.when(pl.program_id(1) == 0)
    def _():
        x = x_ref[...]
        h_ref[...] = (x * _rms_scale(x) * g_ref[...]).astype(BF16)

    h = h_ref[...]
    a = jnp.dot(h, wa_ref[...].astype(BF16), preferred_element_type=F32) + ba_ref[...]
    gate = jnp.dot(h, wg_ref[...].astype(BF16), preferred_element_type=F32) + bg_ref[...]
    u_ref[...] = a * jax.nn.sigmoid(gate)


def _glu(x, g, w, b, *, tm, tn):
    m, d = x.shape
    nj = d // tn
    return pl.pallas_call(
        _glu_kernel,
        grid=(m // tm, nj),
        in_specs=[
            pl.BlockSpec((tm, d), lambda i, j: (i, 0)),
            pl.BlockSpec((1, d), lambda i, j: (0, 0)),
            pl.BlockSpec((d, tn), lambda i, j: (0, j)),
            pl.BlockSpec((d, tn), lambda i, j: (0, j + nj)),
            pl.BlockSpec((1, tn), lambda i, j: (0, j)),
            pl.BlockSpec((1, tn), lambda i, j: (0, j + nj)),
        ],
        out_specs=pl.BlockSpec((tm, tn), lambda i, j: (i, j)),
        out_shape=jax.ShapeDtypeStruct((m, d), F32),
        scratch_shapes=[pltpu.VMEM((tm, d), BF16)],
        compiler_params=_params("parallel", "arbitrary"),
        name="glu",
    )(x, g, w, w, b, b)


def _ln_silu_pw2(c, x, lng_ref, lnb_ref, w2_ref, b2_ref):
    mu = jnp.mean(c, axis=-1, keepdims=True)
    cc = c - mu
    var = jnp.mean(cc * cc, axis=-1, keepdims=True)
    y = cc * lax.rsqrt(var + EPS) * lng_ref[...] + lnb_ref[...]
    y = y * jax.nn.sigmoid(y)
    return x + jnp.dot(y.astype(BF16), w2_ref[...], preferred_element_type=F32) + b2_ref[...]


CONV_ROWS = 64
CONV_LANES = 128


def _conv_prompt_kernel(u_ref, halo_ref, x_ref, wdw_ref, bdw_ref, lng_ref, lnb_ref, w2_ref, b2_ref,
                        o_ref, full_ref, c_ref):
    tt, d = c_ref.shape
    keep = (pl.program_id(1) > 0).astype(F32)
    full_ref[0:HALO, :] = halo_ref[0] * keep
    full_ref[HALO:, :] = u_ref[0]

    def chunk(ci, carry):
        r0 = pl.multiple_of(ci * CONV_ROWS, CONV_ROWS)
        for lc in range(d // CONV_LANES):
            lanes = slice(lc * CONV_LANES, (lc + 1) * CONV_LANES)
            win = full_ref[pl.ds(r0, CONV_ROWS + HALO), lanes]
            acc = jnp.zeros((CONV_ROWS, CONV_LANES), F32)
            for res in range(SUBLANES):
                taps = [k for k in range(CONV_WIDTH) if (TAP0 + k) % SUBLANES == res]
                shifted = win if res == 0 else pltpu.roll(win, CONV_ROWS + HALO - res, 0)
                for k in taps:
                    a = (TAP0 + k) // SUBLANES * SUBLANES
                    acc = acc + shifted[a:a + CONV_ROWS] * wdw_ref[k:k + 1, lanes]
            c_ref[pl.ds(r0, CONV_ROWS), lanes] = acc + bdw_ref[:, lanes]
        return carry

    lax.fori_loop(0, tt // CONV_ROWS, chunk, 0)
    o_ref[0] = _ln_silu_pw2(c_ref[...], x_ref[0], lng_ref, lnb_ref, w2_ref, b2_ref)


def _conv_prompt(u, x, wdw, bdw, lng, lnb, w2, b2, *, tt):
    b, t, d = u.shape
    hb = tt // HALO
    return pl.pallas_call(
        _conv_prompt_kernel,
        grid=(b, t // tt),
        in_specs=[
            pl.BlockSpec((1, tt, d), lambda bi, ti: (bi, ti, 0)),
            pl.BlockSpec((1, HALO, d), lambda bi, ti: (bi, jnp.maximum(ti * hb - 1, 0), 0)),
            pl.BlockSpec((1, tt, d), lambda bi, ti: (bi, ti, 0)),
            _resident((CONV_WIDTH, d)),
            _resident((1, d)),
            _resident((1, d)),
            _resident((1, d)),
            _resident((d, d)),
            _resident((1, d)),
        ],
        out_specs=pl.BlockSpec((1, tt, d), lambda bi, ti: (bi, ti, 0)),
        out_shape=jax.ShapeDtypeStruct((b, t, d), F32),
        scratch_shapes=[pltpu.VMEM((HALO + tt, d), F32), pltpu.VMEM((tt, d), F32)],
        compiler_params=_params("parallel", "arbitrary"),
        name="conv_prompt",
    )(u, u, x, wdw, bdw, lng, lnb, w2, b2)


def _conv_sample_kernel(full_ref, x_ref, wdw_ref, bdw_ref, lng_ref, lnb_ref, w2_ref, b2_ref,
                        o_ref, c_ref):
    nb, rows, d = full_ref.shape
    s = rows - (CONV_WIDTH - 1)
    for lc in range(d // CONV_LANES):
        lanes = slice(lc * CONV_LANES, (lc + 1) * CONV_LANES)
        acc = jnp.zeros((nb, s, CONV_LANES), F32)
        for k in range(CONV_WIDTH):
            acc = acc + full_ref[:, k:k + s, lanes] * wdw_ref[k:k + 1, lanes]
        c_ref[:, lanes] = acc.reshape(nb * s, CONV_LANES) + bdw_ref[:, lanes]
    o_ref[...] = _ln_silu_pw2(c_ref[...], x_ref[...], lng_ref, lnb_ref, w2_ref, b2_ref)


def _conv_sample(full, x, wdw, bdw, lng, lnb, w2, b2):
    m, d = x.shape
    whole = lambda a: pl.BlockSpec(a.shape, lambda i: (0,) * a.ndim)
    args = (full, x, wdw, bdw, lng, lnb, w2, b2)
    return pl.pallas_call(
        _conv_sample_kernel,
        grid=(1,),
        in_specs=[whole(a) for a in args],
        out_specs=pl.BlockSpec((m, d), lambda i: (0, 0)),
        out_shape=jax.ShapeDtypeStruct((m, d), F32),
        scratch_shapes=[pltpu.VMEM((m, d), F32)],
        compiler_params=_params("arbitrary"),
        name="conv_sample",
    )(*args)


def _mlp_kernel(x_ref, g_ref, wu_ref, wd_ref, fg_ref, o_ref, h_ref, *, final):
    f = pl.program_id(1)

    @pl.when(f == 0)
    def _():
        x = x_ref[...]
        h_ref[...] = (x * _rms_scale(x) * g_ref[...]).astype(BF16)
        o_ref[...] = x

    hid = jnp.dot(h_ref[...], wu_ref[...].astype(BF16), preferred_element_type=F32)
    hid = jnp.square(jnp.maximum(hid, 0.0)).astype(BF16)
    o_ref[...] += jnp.dot(hid, wd_ref[...].astype(BF16), preferred_element_type=F32)

    if final:
        @pl.when(f == pl.num_programs(1) - 1)
        def _():
            y = o_ref[...]
            o_ref[...] = y * _rms_scale(y) * fg_ref[...]


def _mlp(x, g, wu, wd, fg, *, layer, final, tm, tf):
    m, d = x.shape
    dff = wu.shape[2]
    return pl.pallas_call(
        functools.partial(_mlp_kernel, final=final),
        grid=(m // tm, dff // tf),
        in_specs=[
            pl.BlockSpec((tm, d), lambda i, f: (i, 0), pipeline_mode=pl.Buffered(1)),
            pl.BlockSpec((1, d), lambda i, f: (0, 0)),
            pl.BlockSpec((None, d, tf), lambda i, f: (layer, 0, f)),
            pl.BlockSpec((None, tf, d), lambda i, f: (layer, f, 0)),
            pl.BlockSpec((1, d), lambda i, f: (0, 0)),
        ],
        out_specs=pl.BlockSpec((tm, d), lambda i, f: (i, 0)),
        out_shape=jax.ShapeDtypeStruct((m, d), F32),
        scratch_shapes=[pltpu.VMEM((tm, d), BF16)],
        compiler_params=_params("parallel", "arbitrary"),
        name="mlp_final" if final else "mlp",
    )(x, g, wu, wd, fg)


def _rope_tables(pos):
    half = ROT_DIM // 2
    inv = ROPE_THETA ** (-jnp.arange(half, dtype=F32) / half)
    ang = pos.astype(F32)[:, None] * inv[None, :]
    cos, sin = lax.optimization_barrier((jnp.cos(ang), jnp.sin(ang)))
    ones = jnp.ones((pos.shape[0], HEAD_DIM - ROT_DIM), F32)
    zeros = jnp.zeros_like(ones)
    zh = jnp.zeros_like(sin)
    per_head = lambda parts: jnp.tile(jnp.concatenate(parts, axis=1), (1, LANES // HEAD_DIM))
    return per_head([cos, cos, ones]), per_head([zh, sin, zeros]), per_head([-sin, zh, zeros])


def _rope_block(blk, cos, sin_up, sin_dn):
    half = ROT_DIM // 2
    return (blk * cos + pltpu.roll(blk, half, 1) * sin_up
            + pltpu.roll(blk, LANES - half, 1) * sin_dn)


def _qkv_kernel(x_ref, gq_ref, gk_ref, w_ref, cos_ref, sup_ref, sdn_ref, q_ref, kv_ref,
                hq_ref, hk_ref, *, nq):
    j = pl.program_id(1)
    tn = w_ref.shape[1]
    k_width = kv_ref.shape[1] // 2

    @pl.when(j == 0)
    def _():
        x = x_ref[...]
        xn = x * _rms_scale(x)
        hq_ref[...] = (xn * gq_ref[...]).astype(BF16)
        hk_ref[...] = (xn * gk_ref[...]).astype(BF16)

    tables = (cos_ref[...], sup_ref[...], sdn_ref[...])

    def project(h_ref, out_ref, rope_cols):
        res = jnp.dot(h_ref[...], w_ref[...].astype(BF16), preferred_element_type=F32)
        for c in range(rope_cols // LANES):
            lanes = slice(c * LANES, (c + 1) * LANES)
            out_ref[:, lanes] = _rope_block(res[:, lanes], *tables)
        if rope_cols < tn:
            out_ref[:, rope_cols:] = res[:, rope_cols:]

    @pl.when(j < nq)
    def _():
        project(hq_ref, q_ref, tn)

    @pl.when(j == nq)
    def _():
        project(hk_ref, kv_ref, k_width)


def _qkv(x, gq, gk, w, tables, *, tm, table_blocks):
    m, d = x.shape
    n_kv = w.shape[1] - d
    tn = n_kv
    nq = d // tn
    tspec = pl.BlockSpec((tm, LANES), lambda i, j: (i % table_blocks, 0))
    return pl.pallas_call(
        functools.partial(_qkv_kernel, nq=nq),
        grid=(m // tm, nq + 1),
        in_specs=[
            pl.BlockSpec((tm, d), lambda i, j: (i, 0)),
            pl.BlockSpec((1, d), lambda i, j: (0, 0)),
            pl.BlockSpec((1, d), lambda i, j: (0, 0)),
            pl.BlockSpec((d, tn), lambda i, j: (0, j)),
            tspec, tspec, tspec,
        ],
        out_specs=[
            pl.BlockSpec((tm, tn), lambda i, j: (i, jnp.minimum(j, nq - 1))),
            pl.BlockSpec((tm, n_kv), lambda i, j: (i, 0)),
        ],
        out_shape=[jax.ShapeDtypeStruct((m, d), F32), jax.ShapeDtypeStruct((m, n_kv), F32)],
        scratch_shapes=[pltpu.VMEM((tm, d), BF16), pltpu.VMEM((tm, d), BF16)],
        compiler_params=_params("parallel", "arbitrary"),
        name="qkv",
    )(x, gq, gk, w, *tables)


def _softmax_pv(s, sink, vv):
    m = jnp.maximum(jnp.max(s, axis=-1, keepdims=True), sink)
    p = jnp.exp(s - m)
    denom = jnp.sum(p, axis=-1, keepdims=True) + jnp.exp(sink - m)
    return jnp.dot(p.astype(BF16), vv, preferred_element_type=F32) * (1.0 / denom)


def _qk(q, k):
    return lax.dot_general(q, k, (((1,), (1,)), ((), ())), preferred_element_type=F32)


def _attn_prompt_kernel(q_ref, kvp_ref, kvc_ref, x_ref, wo_ref, sink_ref, o_ref, kv_ref, ot_ref):
    blk = kvp_ref.shape[1]
    n_blk = q_ref.shape[1] // blk
    k_width = kvc_ref.shape[2] // 2
    group = q_ref.shape[2] // k_width
    n_kv_heads = k_width // HEAD_DIM
    kv_ref[0:blk, :] = kvp_ref[0].astype(BF16)
    kv_ref[blk:, :] = kvc_ref[0].astype(BF16)

    key = lax.broadcasted_iota(jnp.int32, (2 * blk, group * blk), 0)
    qry = lax.broadcasted_iota(jnp.int32, (2 * blk, group * blk), 1) % blk
    own = (key >= blk) & (key - blk <= qry)
    bias_inner = jnp.where(((key < blk) & (key > qry)) | own, 0.0, -jnp.inf).astype(F32)
    bias_first = jnp.where(pl.program_id(1) > 0, bias_inner, jnp.where(own, 0.0, -jnp.inf))
    head_of_lane = lax.broadcasted_iota(jnp.int32, (1, group * blk), 1) // blk
    scale = HEAD_DIM ** -0.5

    def q_block(qb, carry):
        row0 = pl.multiple_of(qb * blk, blk)
        bias = jnp.where(qb == 0, bias_first, bias_inner)

        def scores(j):
            kk = kv_ref[pl.ds(row0, 2 * blk), j * HEAD_DIM:(j + 1) * HEAD_DIM]
            qs = jnp.concatenate(
                [q_ref[0, pl.ds(row0, blk), h * HEAD_DIM:(h + 1) * HEAD_DIM]
                 for h in range(j * group, (j + 1) * group)], axis=0)
            return _qk(kk, (qs * scale).astype(BF16)) + bias

        st = scores(0)
        for j in range(n_kv_heads):
            st_next = scores(j + 1) if j + 1 < n_kv_heads else None
            vv = kv_ref[pl.ds(row0, 2 * blk), k_width + j * HEAD_DIM:k_width + (j + 1) * HEAD_DIM]
            sink = jnp.zeros((1, group * blk), F32)
            for g in range(group):
                sink = jnp.where(head_of_lane == g, sink_ref[j * group + g], sink)
            m = jnp.maximum(jnp.max(st, axis=0, keepdims=True), sink)
            p = jnp.exp(st - m)
            denom = jnp.sum(p, axis=0, keepdims=True) + jnp.exp(sink - m)
            ot = lax.dot_general(vv, p.astype(BF16), (((0,), (0,)), ((), ())),
                                 preferred_element_type=F32) * (1.0 / denom)
            for g in range(group):
                h = j * group + g
                ot_ref[h * HEAD_DIM:(h + 1) * HEAD_DIM, pl.ds(row0, blk)] = (
                    ot[:, g * blk:(g + 1) * blk].astype(BF16))
            st = st_next
        return carry

    lax.fori_loop(0, n_blk, q_block, 0)

    proj = lax.dot_general(ot_ref[...], wo_ref[...], (((0,), (0,)), ((), ())),
                           preferred_element_type=F32)
    o_ref[0] = x_ref[0] + proj


def _attn_prompt(q, kv, x, wo, sinks, *, tq):
    b, t, d = x.shape
    n_kv = kv.shape[2]
    blk = WINDOW
    per = tq // blk
    return pl.pallas_call(
        _attn_prompt_kernel,
        grid=(b, t // tq),
        in_specs=[
            pl.BlockSpec((1, tq, d), lambda bi, n: (bi, n, 0)),
            pl.BlockSpec((1, blk, n_kv), lambda bi, n: (bi, jnp.maximum(n * per - 1, 0), 0)),
            pl.BlockSpec((1, tq, n_kv), lambda bi, n: (bi, n, 0)),
            pl.BlockSpec((1, tq, d), lambda bi, n: (bi, n, 0)),
            _resident((d, d)),
            pl.BlockSpec(memory_space=pltpu.SMEM),
        ],
        out_specs=pl.BlockSpec((1, tq, d), lambda bi, n: (bi, n, 0)),
        out_shape=jax.ShapeDtypeStruct((b, t, d), F32),
        scratch_shapes=[pltpu.VMEM((blk + tq, n_kv), BF16), pltpu.VMEM((d, tq), BF16)],
        compiler_params=_params("parallel", "arbitrary"),
        name="attn_prompt",
    )(q, kv, kv, x, wo, sinks)


def _attn_sample_kernel(q_ref, kvn_ref, ck_ref, cv_ref, x_ref, wo_ref, sink_ref,
                        o_ref, wk_ref, wv_ref, ocat_ref):
    gb, s, d = q_ref.shape
    win, k_width = ck_ref.shape[1], ck_ref.shape[2]
    group = d // k_width
    step = pl.program_id(0)

    wk_ref[:, 0:win - s, :] = ck_ref[:, s:win, :]
    wk_ref[:, win - s:win, :] = kvn_ref[:, :, 0:k_width]
    wv_ref[:, 0:win - s, :] = cv_ref[:, s:win, :]
    wv_ref[:, win - s:win, :] = kvn_ref[:, :, k_width:]

    qi = lax.broadcasted_iota(jnp.int32, (group * s, win + s), 0) % s
    kj = lax.broadcasted_iota(jnp.int32, (group * s, win + s), 1)
    rel = qi + win - kj
    bias = jnp.where((rel >= 0) & (rel < WINDOW), 0.0, -jnp.inf).astype(F32)
    head_of_row = lax.broadcasted_iota(jnp.int32, (group * s, 1), 0) // s
    scale = HEAD_DIM ** -0.5

    def one_batch(bb, carry):
        row0 = pl.multiple_of((step * gb + bb) * s, s)
        qb = q_ref[bb]
        kn = kvn_ref[bb]
        ck = ck_ref[bb]
        cv = cv_ref[bb]
        for j in range(k_width // HEAD_DIM):
            kl = slice(j * HEAD_DIM, (j + 1) * HEAD_DIM)
            vl = slice(k_width + j * HEAD_DIM, k_width + (j + 1) * HEAD_DIM)
            kk = jnp.concatenate([ck[:, kl], kn[:, kl]], axis=0).astype(BF16)
            vv = jnp.concatenate([cv[:, kl], kn[:, vl]], axis=0).astype(BF16)
            heads = [j * group + g for g in range(group)]
            qs = jnp.concatenate([qb[:, h * HEAD_DIM:(h + 1) * HEAD_DIM] for h in heads], axis=0)
            sink = jnp.zeros((group * s, 1), F32)
            for g, h in enumerate(heads):
                sink = jnp.where(head_of_row == g, sink_ref[h], sink)
            o = _softmax_pv(_qk((qs * scale).astype(BF16), kk) + bias, sink, vv)
            for g, h in enumerate(heads):
                ocat_ref[pl.ds(row0, s), h * HEAD_DIM:(h + 1) * HEAD_DIM] = o[g * s:(g + 1) * s]
        return carry

    lax.fori_loop(0, gb, one_batch, 0)

    @pl.when(step == pl.num_programs(0) - 1)
    def _():
        o_ref[...] = x_ref[...] + jnp.dot(ocat_ref[...].astype(BF16), wo_ref[...],
                                          preferred_element_type=F32)


def _attn_sample(q, kvn, ck, cv, x, wo, sinks, *, gb):
    nb, s, d = q.shape
    win, k_width = ck.shape[1], ck.shape[2]
    m = nb * s
    cache_spec = pl.BlockSpec((gb, win, k_width), lambda i: (i, 0, 0))
    return pl.pallas_call(
        _attn_sample_kernel,
        grid=(nb // gb,),
        in_specs=[
            pl.BlockSpec((gb, s, d), lambda i: (i, 0, 0)),
            pl.BlockSpec((gb, s, 2 * k_width), lambda i: (i, 0, 0)),
            cache_spec,
            cache_spec,
            _resident((m, d)),
            _resident((d, d)),
            pl.BlockSpec(memory_space=pltpu.SMEM),
        ],
        out_specs=[pl.BlockSpec((m, d), lambda i: (0, 0)), cache_spec, cache_spec],
        out_shape=[
            jax.ShapeDtypeStruct((m, d), F32),
            jax.ShapeDtypeStruct(ck.shape, F32),
            jax.ShapeDtypeStruct(cv.shape, F32),
        ],
        scratch_shapes=[pltpu.VMEM((m, d), F32)],
        compiler_params=_params("arbitrary"),
        name="attn_sample",
    )(q, kvn, ck, cv, x, wo, sinks)


def _row(v):
    return v.reshape(1, -1)


def _tile(m, want):
    return want if m % want == 0 else m


def kernel(x_prompt, x_sample, state_conv, cache_k, cache_v, norm_mix, w_pw1, b_pw1, w_dw, b_dw,
           conv_ln_g, conv_ln_b, w_pw2, b_pw2, kv_norm, w_k, w_v, w_q, w_o, sinks, norm_mlp,
           w_up, w_down, final_norm):
    b, t, d = x_prompt.shape
    nb, s, _ = x_sample.shape
    win = cache_k.shape[1]
    k_width = cache_k.shape[2] * cache_k.shape[3]
    depth = norm_mlp.shape[0]
    n_conv = w_pw1.shape[0]
    assert depth == 2 and n_conv == 1 and w_q.shape[0] == 1, "one conv layer then one attention layer"
    assert t % WINDOW == 0 and win == WINDOW

    w_pw2b = w_pw2[0].astype(BF16)
    w_ob = w_o[0].astype(BF16)
    w_qkv = jnp.concatenate([w_q[0], w_k, w_v], axis=1)

    def tokens(x, *, conv, attend, tables, table_blocks, tm_qkv):
        m = x.shape[0]
        tm = _tile(m, 1024)
        u = _glu(x, _row(norm_mix[0]), w_pw1[0], _row(b_pw1[0]), tm=tm, tn=512)
        x, conv_state = conv(u, x)
        x = _mlp(x, _row(norm_mlp[0]), w_up, w_down, _row(final_norm),
                 layer=0, final=False, tm=tm, tf=512)
        q, kv = _qkv(x, _row(norm_mix[1]), _row(kv_norm), w_qkv, tables, tm=tm_qkv,
                     table_blocks=table_blocks)
        x, extra = attend(q, kv, x)
        y = _mlp(x, _row(norm_mlp[1]), w_up, w_down, _row(final_norm),
                 layer=1, final=True, tm=tm, tf=512)
        return y, conv_state, kv, extra

    conv_w = (w_dw[0], _row(b_dw[0]), _row(conv_ln_g[0]), _row(conv_ln_b[0]), w_pw2b,
              _row(b_pw2[0]))

    def conv_p(u, x):
        u3 = u.reshape(b, t, d)
        x3 = _conv_prompt(u3, x.reshape(b, t, d), *conv_w, tt=256)
        return x3.reshape(b * t, d), u3[:, t - (CONV_WIDTH - 1):]

    def attend_p(q, kv, x):
        x3 = _attn_prompt(q.reshape(b, t, d), kv.reshape(b, t, -1), x.reshape(b, t, d), w_ob,
                          sinks[0], tq=512)
        return x3.reshape(b * t, d), None

    tm_p = _tile(b * t, 512)
    y_p, conv_p_state, kv_p, _ = tokens(
        x_prompt.reshape(b * t, d), conv=conv_p, attend=attend_p,
        tables=_rope_tables(jnp.arange(t, dtype=jnp.int32)), table_blocks=t // tm_p, tm_qkv=tm_p)
    kv_p = kv_p.reshape(b, t, -1)[:, t - WINDOW:].reshape(b, WINDOW, 2, N_KV_HEADS, HEAD_DIM)

    pos_s = PAST_LEN + jnp.arange(s, dtype=jnp.int32)

    def conv_s(u, x):
        full = jnp.concatenate([state_conv[0], u.reshape(nb, s, d)], axis=1)
        return _conv_sample(full, x, *conv_w), full[:, -(CONV_WIDTH - 1):]

    def attend_s(q, kv, x):
        x, wk, wv = _attn_sample(q.reshape(nb, s, d), kv.reshape(nb, s, -1),
                                 cache_k.reshape(nb, win, k_width),
                                 cache_v.reshape(nb, win, k_width), x, w_ob, sinks[0], gb=8)
        return x, (wk.reshape(cache_k.shape), wv.reshape(cache_v.shape))

    tables_s = tuple(jnp.tile(tb, (nb, 1)) for tb in _rope_tables(pos_s))
    y_s, conv_s_state, _, (win_k_s, win_v_s) = tokens(
        x_sample.reshape(nb * s, d), conv=conv_s, attend=attend_s, tables=tables_s,
        table_blocks=1, tm_qkv=nb * s)

    return (y_p.reshape(b, t, d), y_s.reshape(nb, s, d), conv_p_state[None], conv_s_state[None],
            kv_p[:, :, 0], kv_p[:, :, 1], win_k_s, win_v_s)
```

```python
import functools

import jax
import jax.numpy as jnp
from jax import lax
from jax.experimental import pallas as pl
from jax.experimental.pallas import tpu as pltpu

F32 = jnp.float32
BF16 = jnp.bfloat16

EPS = 1e-6
HEAD_DIM = 64
N_KV_HEADS = 8
ROT_DIM = HEAD_DIM // 4
ROPE_THETA = 500000.0
WINDOW = 128
CONV_WIDTH = 31
PAST_LEN = 16384

LANES = 128
SUBLANES = 8
VMEM_LIMIT_BYTES = 54 * 1024 * 1024

HALO = -(-(CONV_WIDTH - 1) // SUBLANES) * SUBLANES
TAP0 = HALO - (CONV_WIDTH - 1)


def _params(*semantics):
    return pltpu.CompilerParams(dimension_semantics=semantics, vmem_limit_bytes=VMEM_LIMIT_BYTES)


def _resident(shape):
    return pl.BlockSpec(shape, lambda *_: (0,) * len(shape), pipeline_mode=pl.Buffered(1))


def _rms_scale(x):
    return lax.rsqrt(jnp.mean(x * x, axis=-1, keepdims=True) + EPS)


def _glu_kernel(x_ref, g_ref, wa_ref, wg_ref, ba_ref, bg_ref, u_ref, h_ref):
    @pl.when(pl.program_id(1) == 0)
    def _():
        x = x_ref[...]
        h_ref[...] = (x * _rms_scale(x) * g_ref[...]).astype(BF16)

    h = h_ref[...]
    a = jnp.dot(h, wa_ref[...].astype(BF16), preferred_element_type=F32) + ba_ref[...]
    gate = jnp.dot(h, wg_ref[...].astype(BF16), preferred_element_type=F32) + bg_ref[...]
    u_ref[...] = a * jax.nn.sigmoid(gate)


def _glu(x, g, w, b, *, tm, tn):
    m, d = x.shape
    nj = d // tn
    return pl.pallas_call(
        _glu_kernel,
        grid=(m // tm, nj),
        in_specs=[
            pl.BlockSpec((tm, d), lambda i, j: (i, 0)),
            pl.BlockSpec((1, d), lambda i, j: (0, 0)),
            pl.BlockSpec((d, tn), lambda i, j: (0, j)),
            pl.BlockSpec((d, tn), lambda i, j: (0, j + nj)),
            pl.BlockSpec((1, tn), lambda i, j: (0, j)),
            pl.BlockSpec((1, tn), lambda i, j: (0, j + nj)),
        ],
        out_specs=pl.BlockSpec((tm, tn), lambda i, j: (i, j)),
        out_shape=jax.ShapeDtypeStruct((m, d), F32),
        scratch_shapes=[pltpu.VMEM((tm, d), BF16)],
        compiler_params=_params("parallel", "arbitrary"),
        name="glu",
    )(x, g, w, w, b, b)


def _ln_silu_pw2(c, x, lng_ref, lnb_ref, w2_ref, b2_ref):
    mu = jnp.mean(c, axis=-1, keepdims=True)
    cc = c - mu
    var = jnp.mean(cc * cc, axis=-1, keepdims=True)
    y = cc * lax.rsqrt(var + EPS) * lng_ref[...] + lnb_ref[...]
    y = y * jax.nn.sigmoid(y)
    return x + jnp.dot(y.astype(BF16), w2_ref[...], preferred_element_type=F32) + b2_ref[...]


CONV_ROWS = 64
CONV_LANES = 128


def _conv_taps(win, wdw_ref, lanes):
    acc = jnp.zeros((CONV_ROWS, CONV_LANES), F32)
    for res in range(SUBLANES):
        shifted = win if res == 0 else pltpu.roll(win, CONV_ROWS + HALO - res, 0)
        for k in range(CONV_WIDTH):
            if (TAP0 + k) % SUBLANES == res:
                a = (TAP0 + k) // SUBLANES * SUBLANES
                acc = acc + shifted[a:a + CONV_ROWS] * wdw_ref[k:k + 1, lanes]
    return acc


def _glu_conv_kernel(x_ref, g_ref, w_ref, b_ref, wdw_ref, bdw_ref, c_ref, ulast_ref,
                     h_ref, win_a, win_b, hal_ref, *, tiles_per_seq):
    tm, d = x_ref.shape
    tn = win_a.shape[1]
    nj = d // tn
    x = x_ref[...]
    h_ref[...] = (x * _rms_scale(x) * g_ref[...]).astype(BF16)
    first = pl.program_id(0) % tiles_per_seq == 0

    @pl.when(pl.program_id(0) == 0)
    def _():
        hal_ref[...] = jnp.zeros_like(hal_ref)

    def glu(j, win_ref):
        cols = pl.ds(pl.multiple_of(j * tn, tn), tn)
        gcols = pl.ds(pl.multiple_of(d + j * tn, tn), tn)
        a = jnp.dot(h_ref[...], w_ref[:, cols], preferred_element_type=F32) + b_ref[:, cols]
        gate = jnp.dot(h_ref[...], w_ref[:, gcols], preferred_element_type=F32) + b_ref[:, gcols]
        u = a * jax.nn.sigmoid(gate)
        tail = u[tm - HALO:, :]
        win_ref[0:HALO, :] = jnp.where(first, 0.0, hal_ref[:, cols])
        win_ref[HALO:, :] = u
        hal_ref[:, cols] = tail
        ulast_ref[0, :, cols] = tail

    def conv(j, win_ref):
        for rc in range(tm // CONV_ROWS):
            for lc in range(tn // CONV_LANES):
                lanes = slice(lc * CONV_LANES, (lc + 1) * CONV_LANES)
                out_lanes = pl.ds(pl.multiple_of(j * tn + lc * CONV_LANES, CONV_LANES), CONV_LANES)
                win = win_ref[rc * CONV_ROWS:(rc + 1) * CONV_ROWS + HALO, lanes]
                c_ref[rc * CONV_ROWS:(rc + 1) * CONV_ROWS, out_lanes] = (
                    _conv_taps(win, wdw_ref, out_lanes) + bdw_ref[:, out_lanes])

    def tile_pair(p, carry):
        glu(2 * p + 1, win_b)
        conv(2 * p, win_a)
        glu(2 * p + 2, win_a)
        conv(2 * p + 1, win_b)
        return carry

    glu(0, win_a)
    lax.fori_loop(0, nj // 2 - 1, tile_pair, 0)
    glu(nj - 1, win_b)
    conv(nj - 2, win_a)
    conv(nj - 1, win_b)


def _glu_conv(x, g, w, b, wdw, bdw, *, seq_len, tm, tn):
    m, d = x.shape
    tiles_per_seq = seq_len // tm
    assert seq_len % tm == 0 and (d // tn) % 2 == 0
    return pl.pallas_call(
        functools.partial(_glu_conv_kernel, tiles_per_seq=tiles_per_seq),
        grid=(m // tm,),
        in_specs=[
            pl.BlockSpec((tm, d), lambda i: (i, 0)),
            _resident((1, d)),
            _resident((d, 2 * d)),
            _resident((1, 2 * d)),
            _resident((CONV_WIDTH, d)),
            _resident((1, d)),
        ],
        out_specs=[
            pl.BlockSpec((tm, d), lambda i: (i, 0)),
            pl.BlockSpec((1, HALO, d), lambda i: (i // tiles_per_seq, 0, 0)),
        ],
        out_shape=[jax.ShapeDtypeStruct((m, d), F32),
                   jax.ShapeDtypeStruct((m // seq_len, HALO, d), F32)],
        scratch_shapes=[pltpu.VMEM((tm, d), BF16), pltpu.VMEM((HALO + tm, tn), F32),
                        pltpu.VMEM((HALO + tm, tn), F32), pltpu.VMEM((HALO, d), F32)],
        compiler_params=_params("arbitrary"),
        name="glu_conv",
    )(x, g, w, b, wdw, bdw)


def _ln_pw2_kernel(c_ref, x_ref, lng_ref, lnb_ref, w2_ref, b2_ref, o_ref):
    o_ref[...] = _ln_silu_pw2(c_ref[...], x_ref[...], lng_ref, lnb_ref, w2_ref, b2_ref)


def _ln_pw2(c, x, lng, lnb, w2, b2, *, tm):
    m, d = x.shape
    rows = pl.BlockSpec((tm, d), lambda i: (i, 0))
    return pl.pallas_call(
        _ln_pw2_kernel,
        grid=(m // tm,),
        in_specs=[rows, rows, _resident((1, d)), _resident((1, d)), _resident((d, d)),
                  _resident((1, d))],
        out_specs=rows,
        out_shape=jax.ShapeDtypeStruct((m, d), F32),
        compiler_params=_params("parallel"),
        name="ln_pw2",
    )(c, x, lng, lnb, w2, b2)


def _conv_sample_kernel(full_ref, x_ref, wdw_ref, bdw_ref, lng_ref, lnb_ref, w2_ref, b2_ref,
                        o_ref, c_ref):
    nb, rows, d = full_ref.shape
    s = rows - (CONV_WIDTH - 1)
    for lc in range(d // CONV_LANES):
        lanes = slice(lc * CONV_LANES, (lc + 1) * CONV_LANES)
        acc = jnp.zeros((nb, s, CONV_LANES), F32)
        for k in range(CONV_WIDTH):
            acc = acc + full_ref[:, k:k + s, lanes] * wdw_ref[k:k + 1, lanes]
        c_ref[:, lanes] = acc.reshape(nb * s, CONV_LANES) + bdw_ref[:, lanes]
    o_ref[...] = _ln_silu_pw2(c_ref[...], x_ref[...], lng_ref, lnb_ref, w2_ref, b2_ref)


def _conv_sample(full, x, wdw, bdw, lng, lnb, w2, b2):
    m, d = x.shape
    whole = lambda a: pl.BlockSpec(a.shape, lambda i: (0,) * a.ndim)
    args = (full, x, wdw, bdw, lng, lnb, w2, b2)
    return pl.pallas_call(
        _conv_sample_kernel,
        grid=(1,),
        in_specs=[whole(a) for a in args],
        out_specs=pl.BlockSpec((m, d), lambda i: (0, 0)),
        out_shape=jax.ShapeDtypeStruct((m, d), F32),
        scratch_shapes=[pltpu.VMEM((m, d), F32)],
        compiler_params=_params("arbitrary"),
        name="conv_sample",
    )(*args)


def _mlp_kernel(x_ref, g_ref, wu_ref, wd_ref, fg_ref, o_ref, h_ref, *, final):
    f = pl.program_id(1)

    @pl.when(f == 0)
    def _():
        x = x_ref[...]
        h_ref[...] = (x * _rms_scale(x) * g_ref[...]).astype(BF16)
        o_ref[...] = x

    hid = jnp.dot(h_ref[...], wu_ref[...].astype(BF16), preferred_element_type=F32)
    hid = jnp.square(jnp.maximum(hid, 0.0)).astype(BF16)
    o_ref[...] += jnp.dot(hid, wd_ref[...].astype(BF16), preferred_element_type=F32)

    if final:
        @pl.when(f == pl.num_programs(1) - 1)
        def _():
            y = o_ref[...]
            o_ref[...] = y * _rms_scale(y) * fg_ref[...]


def _mlp(x, g, wu, wd, fg, *, layer, final, tm, tf):
    m, d = x.shape
    dff = wu.shape[2]
    return pl.pallas_call(
        functools.partial(_mlp_kernel, final=final),
        grid=(m // tm, dff // tf),
        in_specs=[
            pl.BlockSpec((tm, d), lambda i, f: (i, 0), pipeline_mode=pl.Buffered(1)),
            pl.BlockSpec((1, d), lambda i, f: (0, 0)),
            pl.BlockSpec((None, d, tf), lambda i, f: (layer, 0, f)),
            pl.BlockSpec((None, tf, d), lambda i, f: (layer, f, 0)),
            pl.BlockSpec((1, d), lambda i, f: (0, 0)),
        ],
        out_specs=pl.BlockSpec((tm, d), lambda i, f: (i, 0)),
        out_shape=jax.ShapeDtypeStruct((m, d), F32),
        scratch_shapes=[pltpu.VMEM((tm, d), BF16)],
        compiler_params=_params("parallel", "arbitrary"),
        name="mlp_final" if final else "mlp",
    )(x, g, wu, wd, fg)


def _rope_tables(pos):
    half = ROT_DIM // 2
    inv = ROPE_THETA ** (-jnp.arange(half, dtype=F32) / half)
    ang = pos.astype(F32)[:, None] * inv[None, :]
    cos, sin = lax.optimization_barrier((jnp.cos(ang), jnp.sin(ang)))
    ones = jnp.ones((pos.shape[0], HEAD_DIM - ROT_DIM), F32)
    zeros = jnp.zeros_like(ones)
    zh = jnp.zeros_like(sin)
    per_head = lambda parts: jnp.tile(jnp.concatenate(parts, axis=1), (1, LANES // HEAD_DIM))
    return per_head([cos, cos, ones]), per_head([zh, sin, zeros]), per_head([-sin, zh, zeros])


def _rope_block(blk, cos, sin_up, sin_dn):
    half = ROT_DIM // 2
    return (blk * cos + pltpu.roll(blk, half, 1) * sin_up
            + pltpu.roll(blk, LANES - half, 1) * sin_dn)


def _qkv_kernel(x_ref, gq_ref, gk_ref, w_ref, cos_ref, sup_ref, sdn_ref, q_ref, kv_ref,
                hq_ref, hk_ref, *, nq):
    j = pl.program_id(1)
    tn = w_ref.shape[1]
    k_width = kv_ref.shape[1] // 2

    @pl.when(j == 0)
    def _():
        x = x_ref[...]
        xn = x * _rms_scale(x)
        hq_ref[...] = (xn * gq_ref[...]).astype(BF16)
        hk_ref[...] = (xn * gk_ref[...]).astype(BF16)

    tables = (cos_ref[...], sup_ref[...], sdn_ref[...])

    def project(h_ref, out_ref, rope_cols):
        res = jnp.dot(h_ref[...], w_ref[...].astype(BF16), preferred_element_type=F32)
        for c in range(rope_cols // LANES):
            lanes = slice(c * LANES, (c + 1) * LANES)
            out_ref[:, lanes] = _rope_block(res[:, lanes], *tables)
        if rope_cols < tn:
            out_ref[:, rope_cols:] = res[:, rope_cols:]

    @pl.when(j < nq)
    def _():
        project(hq_ref, q_ref, tn)

    @pl.when(j == nq)
    def _():
        project(hk_ref, kv_ref, k_width)


def _qkv(x, gq, gk, w, tables, *, tm, table_blocks):
    m, d = x.shape
    n_kv = w.shape[1] - d
    tn = n_kv
    nq = d // tn
    tspec = pl.BlockSpec((tm, LANES), lambda i, j: (i % table_blocks, 0))
    return pl.pallas_call(
        functools.partial(_qkv_kernel, nq=nq),
        grid=(m // tm, nq + 1),
        in_specs=[
            pl.BlockSpec((tm, d), lambda i, j: (i, 0)),
            pl.BlockSpec((1, d), lambda i, j: (0, 0)),
            pl.BlockSpec((1, d), lambda i, j: (0, 0)),
            pl.BlockSpec((d, tn), lambda i, j: (0, j)),
            tspec, tspec, tspec,
        ],
        out_specs=[
            pl.BlockSpec((tm, tn), lambda i, j: (i, jnp.minimum(j, nq - 1))),
            pl.BlockSpec((tm, n_kv), lambda i, j: (i, 0)),
        ],
        out_shape=[jax.ShapeDtypeStruct((m, d), F32), jax.ShapeDtypeStruct((m, n_kv), F32)],
        scratch_shapes=[pltpu.VMEM((tm, d), BF16), pltpu.VMEM((tm, d), BF16)],
        compiler_params=_params("parallel", "arbitrary"),
        name="qkv",
    )(x, gq, gk, w, *tables)


def _softmax_pv(s, sink, vv):
    m = jnp.maximum(jnp.max(s, axis=-1, keepdims=True), sink)
    p = jnp.exp(s - m)
    denom = jnp.sum(p, axis=-1, keepdims=True) + jnp.exp(sink - m)
    return jnp.dot(p.astype(BF16), vv, preferred_element_type=F32) * (1.0 / denom)


def _qk(q, k):
    return lax.dot_general(q, k, (((1,), (1,)), ((), ())), preferred_element_type=F32)


def _attn_prompt_kernel(q_ref, kvp_ref, kvc_ref, x_ref, wo_ref, sink_ref, o_ref, kv_ref, ot_ref):
    blk = kvp_ref.shape[1]
    n_blk = q_ref.shape[1] // blk
    k_width = kvc_ref.shape[2] // 2
    group = q_ref.shape[2] // k_width
    n_kv_heads = k_width // HEAD_DIM
    kv_ref[0:blk, :] = kvp_ref[0].astype(BF16)
    kv_ref[blk:, :] = kvc_ref[0].astype(BF16)

    key = lax.broadcasted_iota(jnp.int32, (2 * blk, group * blk), 0)
    qry = lax.broadcasted_iota(jnp.int32, (2 * blk, group * blk), 1) % blk
    own = (key >= blk) & (key - blk <= qry)
    bias_inner = jnp.where(((key < blk) & (key > qry)) | own, 0.0, -jnp.inf).astype(F32)
    bias_first = jnp.where(pl.program_id(1) > 0, bias_inner, jnp.where(own, 0.0, -jnp.inf))
    head_of_lane = lax.broadcasted_iota(jnp.int32, (1, group * blk), 1) // blk
    scale = HEAD_DIM ** -0.5

    def q_block(qb, carry):
        row0 = pl.multiple_of(qb * blk, blk)
        bias = jnp.where(qb == 0, bias_first, bias_inner)

        def scores(j):
            kk = kv_ref[pl.ds(row0, 2 * blk), j * HEAD_DIM:(j + 1) * HEAD_DIM]
            qs = jnp.concatenate(
                [q_ref[0, pl.ds(row0, blk), h * HEAD_DIM:(h + 1) * HEAD_DIM]
                 for h in range(j * group, (j + 1) * group)], axis=0)
            return _qk(kk, (qs * scale).astype(BF16)) + bias

        st = scores(0)
        for j in range(n_kv_heads):
            st_next = scores(j + 1) if j + 1 < n_kv_heads else None
            vv = kv_ref[pl.ds(row0, 2 * blk), k_width + j * HEAD_DIM:k_width + (j + 1) * HEAD_DIM]
            sink = jnp.zeros((1, group * blk), F32)
            for g in range(group):
                sink = jnp.where(head_of_lane == g, sink_ref[j * group + g], sink)
            m = jnp.maximum(jnp.max(st, axis=0, keepdims=True), sink)
            p = jnp.exp(st - m)
            denom = jnp.sum(p, axis=0, keepdims=True) + jnp.exp(sink - m)
            ot = lax.dot_general(vv, p.astype(BF16), (((0,), (0,)), ((), ())),
                                 preferred_element_type=F32) * (1.0 / denom)
            for g in range(group):
                h = j * group + g
                ot_ref[h * HEAD_DIM:(h + 1) * HEAD_DIM, pl.ds(row0, blk)] = (
                    ot[:, g * blk:(g + 1) * blk].astype(BF16))
            st = st_next
        return carry

    lax.fori_loop(0, n_blk, q_block, 0)

    proj = lax.dot_general(ot_ref[...], wo_ref[...], (((0,), (0,)), ((), ())),
                           preferred_element_type=F32)
    o_ref[0] = x_ref[0] + proj


def _attn_prompt(q, kv, x, wo, sinks, *, tq):
    b, t, d = x.shape
    n_kv = kv.shape[2]
    blk = WINDOW
    per = tq // blk
    return pl.pallas_call(
        _attn_prompt_kernel,
        grid=(b, t // tq),
        in_specs=[
            pl.BlockSpec((1, tq, d), lambda bi, n: (bi, n, 0)),
            pl.BlockSpec((1, blk, n_kv), lambda bi, n: (bi, jnp.maximum(n * per - 1, 0), 0)),
            pl.BlockSpec((1, tq, n_kv), lambda bi, n: (bi, n, 0)),
            pl.BlockSpec((1, tq, d), lambda bi, n: (bi, n, 0)),
            _resident((d, d)),
            pl.BlockSpec(memory_space=pltpu.SMEM),
        ],
        out_specs=pl.BlockSpec((1, tq, d), lambda bi, n: (bi, n, 0)),
        out_shape=jax.ShapeDtypeStruct((b, t, d), F32),
        scratch_shapes=[pltpu.VMEM((blk + tq, n_kv), BF16), pltpu.VMEM((d, tq), BF16)],
        compiler_params=_params("parallel", "arbitrary"),
        name="attn_prompt",
    )(q, kv, kv, x, wo, sinks)


def _attn_sample_kernel(q_ref, kvn_ref, ck_ref, cv_ref, x_ref, wo_ref, sink_ref,
                        o_ref, wk_ref, wv_ref, ocat_ref):
    gb, s, d = q_ref.shape
    win, k_width = ck_ref.shape[1], ck_ref.shape[2]
    group = d // k_width
    step = pl.program_id(0)

    wk_ref[:, 0:win - s, :] = ck_ref[:, s:win, :]
    wk_ref[:, win - s:win, :] = kvn_ref[:, :, 0:k_width]
    wv_ref[:, 0:win - s, :] = cv_ref[:, s:win, :]
    wv_ref[:, win - s:win, :] = kvn_ref[:, :, k_width:]

    qi = lax.broadcasted_iota(jnp.int32, (group * s, win + s), 0) % s
    kj = lax.broadcasted_iota(jnp.int32, (group * s, win + s), 1)
    rel = qi + win - kj
    bias = jnp.where((rel >= 0) & (rel < WINDOW), 0.0, -jnp.inf).astype(F32)
    head_of_row = lax.broadcasted_iota(jnp.int32, (group * s, 1), 0) // s
    scale = HEAD_DIM ** -0.5

    def one_batch(bb, carry):
        row0 = pl.multiple_of((step * gb + bb) * s, s)
        qb = q_ref[bb]
        kn = kvn_ref[bb]
        ck = ck_ref[bb]
        cv = cv_ref[bb]
        n_kv_heads = k_width // HEAD_DIM

        def scores(j):
            kl = slice(j * HEAD_DIM, (j + 1) * HEAD_DIM)
            kk = jnp.concatenate([ck[:, kl], kn[:, kl]], axis=0).astype(BF16)
            qs = jnp.concatenate([qb[:, h * HEAD_DIM:(h + 1) * HEAD_DIM]
                                  for h in range(j * group, (j + 1) * group)], axis=0)
            return _qk((qs * scale).astype(BF16), kk) + bias

        sc = scores(0)
        for j in range(n_kv_heads):
            sc_next = scores(j + 1) if j + 1 < n_kv_heads else None
            kl = slice(j * HEAD_DIM, (j + 1) * HEAD_DIM)
            vl = slice(k_width + j * HEAD_DIM, k_width + (j + 1) * HEAD_DIM)
            vv = jnp.concatenate([cv[:, kl], kn[:, vl]], axis=0).astype(BF16)
            heads = [j * group + g for g in range(group)]
            sink = jnp.zeros((group * s, 1), F32)
            for g, h in enumerate(heads):
                sink = jnp.where(head_of_row == g, sink_ref[h], sink)
            o = _softmax_pv(sc, sink, vv)
            for g, h in enumerate(heads):
                ocat_ref[pl.ds(row0, s), h * HEAD_DIM:(h + 1) * HEAD_DIM] = o[g * s:(g + 1) * s]
            sc = sc_next
        return carry

    lax.fori_loop(0, gb, one_batch, 0)

    @pl.when(step == pl.num_programs(0) - 1)
    def _():
        o_ref[...] = x_ref[...] + jnp.dot(ocat_ref[...].astype(BF16), wo_ref[...],
                                          preferred_element_type=F32)


def _attn_sample(q, kvn, ck, cv, x, wo, sinks, *, gb):
    nb, s, d = q.shape
    win, k_width = ck.shape[1], ck.shape[2]
    m = nb * s
    cache_spec = pl.BlockSpec((gb, win, k_width), lambda i: (i, 0, 0))
    return pl.pallas_call(
        _attn_sample_kernel,
        grid=(nb // gb,),
        in_specs=[
            pl.BlockSpec((gb, s, d), lambda i: (i, 0, 0)),
            pl.BlockSpec((gb, s, 2 * k_width), lambda i: (i, 0, 0)),
            cache_spec,
            cache_spec,
            _resident((m, d)),
            _resident((d, d)),
            pl.BlockSpec(memory_space=pltpu.SMEM),
        ],
        out_specs=[pl.BlockSpec((m, d), lambda i: (0, 0)), cache_spec, cache_spec],
        out_shape=[
            jax.ShapeDtypeStruct((m, d), F32),
            jax.ShapeDtypeStruct(ck.shape, F32),
            jax.ShapeDtypeStruct(cv.shape, F32),
        ],
        scratch_shapes=[pltpu.VMEM((m, d), F32)],
        compiler_params=_params("arbitrary"),
        name="attn_sample",
    )(q, kvn, ck, cv, x, wo, sinks)


def _row(v):
    return v.reshape(1, -1)


def _tile(m, want):
    return want if m % want == 0 else m


def kernel(x_prompt, x_sample, state_conv, cache_k, cache_v, norm_mix, w_pw1, b_pw1, w_dw, b_dw,
           conv_ln_g, conv_ln_b, w_pw2, b_pw2, kv_norm, w_k, w_v, w_q, w_o, sinks, norm_mlp,
           w_up, w_down, final_norm):
    b, t, d = x_prompt.shape
    nb, s, _ = x_sample.shape
    win = cache_k.shape[1]
    k_width = cache_k.shape[2] * cache_k.shape[3]
    depth = norm_mlp.shape[0]
    n_conv = w_pw1.shape[0]
    assert depth == 2 and n_conv == 1 and w_q.shape[0] == 1, "one conv layer then one attention layer"
    assert t % WINDOW == 0 and win == WINDOW

    w_pw1b = w_pw1[0].astype(BF16)
    w_pw2b = w_pw2[0].astype(BF16)
    w_ob = w_o[0].astype(BF16)
    w_qkvb = jnp.concatenate([w_q[0], w_k, w_v], axis=1).astype(BF16)
    ln_pw2_w = (_row(conv_ln_g[0]), _row(conv_ln_b[0]), w_pw2b, _row(b_pw2[0]))

    def tokens(x, *, mix, attend, tables, table_blocks, tm_qkv):
        m = x.shape[0]
        tm = _tile(m, 1024)
        x, conv_state = mix(x)
        x = _mlp(x, _row(norm_mlp[0]), w_up, w_down, _row(final_norm),
                 layer=0, final=False, tm=tm, tf=512)
        q, kv = _qkv(x, _row(norm_mix[1]), _row(kv_norm), w_qkvb, tables, tm=tm_qkv,
                     table_blocks=table_blocks)
        x, extra = attend(q, kv, x)
        y = _mlp(x, _row(norm_mlp[1]), w_up, w_down, _row(final_norm),
                 layer=1, final=True, tm=tm, tf=512)
        return y, conv_state, kv, extra

    def mix_p(x):
        c, u_last = _glu_conv(x, _row(norm_mix[0]), w_pw1b, _row(b_pw1[0]), w_dw[0], _row(b_dw[0]),
                              seq_len=t, tm=512, tn=256)
        return _ln_pw2(c, x, *ln_pw2_w, tm=512), u_last[:, HALO - (CONV_WIDTH - 1):]

    def attend_p(q, kv, x):
        x3 = _attn_prompt(q.reshape(b, t, d), kv.reshape(b, t, -1), x.reshape(b, t, d), w_ob,
                          sinks[0], tq=512)
        return x3.reshape(b * t, d), None

    tm_p = _tile(b * t, 512)
    y_p, conv_p_state, kv_p, _ = tokens(
        x_prompt.reshape(b * t, d), mix=mix_p, attend=attend_p,
        tables=_rope_tables(jnp.arange(t, dtype=jnp.int32)), table_blocks=t // tm_p, tm_qkv=tm_p)
    kv_p = kv_p.reshape(b, t, -1)[:, t - WINDOW:].reshape(b, WINDOW, 2, N_KV_HEADS, HEAD_DIM)

    pos_s = PAST_LEN + jnp.arange(s, dtype=jnp.int32)

    def mix_s(x):
        u = _glu(x, _row(norm_mix[0]), w_pw1b, _row(b_pw1[0]), tm=nb * s, tn=512)
        full = jnp.concatenate([state_conv[0], u.reshape(nb, s, d)], axis=1)
        x = _conv_sample(full, x, w_dw[0], _row(b_dw[0]), *ln_pw2_w)
        return x, full[:, -(CONV_WIDTH - 1):]

    def attend_s(q, kv, x):
        x, wk, wv = _attn_sample(q.reshape(nb, s, d), kv.reshape(nb, s, -1),
                                 cache_k.reshape(nb, win, k_width),
                                 cache_v.reshape(nb, win, k_width), x, w_ob, sinks[0], gb=8)
        return x, (wk.reshape(cache_k.shape), wv.reshape(cache_v.shape))

    tables_s = tuple(jnp.tile(tb, (nb, 1)) for tb in _rope_tables(pos_s))
    y_s, conv_s_state, _, (win_k_s, win_v_s) = tokens(
        x_sample.reshape(nb * s, d), mix=mix_s, attend=attend_s, tables=tables_s,
        table_blocks=1, tm_qkv=nb * s)

    return (y_p.reshape(b, t, d), y_s.reshape(nb, s, d), conv_p_state[None], conv_s_state[None],
            kv_p[:, :, 0], kv_p[:, :, 1], win_k_s, win_v_s)
```

```python
import functools

import jax
import jax.numpy as jnp
from jax import lax
from jax.experimental import pallas as pl
from jax.experimental.pallas import tpu as pltpu

F32 = jnp.float32
BF16 = jnp.bfloat16

EPS = 1e-6
HEAD_DIM = 64
N_KV_HEADS = 8
ROT_DIM = HEAD_DIM // 4
ROPE_THETA = 500000.0
WINDOW = 128
CONV_WIDTH = 31
PAST_LEN = 16384

LANES = 128
SUBLANES = 8
VMEM_LIMIT_BYTES = 54 * 1024 * 1024

HALO = -(-(CONV_WIDTH - 1) // SUBLANES) * SUBLANES
TAP0 = HALO - (CONV_WIDTH - 1)


def _params(*semantics):
    return pltpu.CompilerParams(dimension_semantics=semantics, vmem_limit_bytes=VMEM_LIMIT_BYTES)


def _resident(shape):
    return pl.BlockSpec(shape, lambda *_: (0,) * len(shape), pipeline_mode=pl.Buffered(1))


def _rms_scale(x):
    return lax.rsqrt(jnp.mean(x * x, axis=-1, keepdims=True) + EPS)


def _glu_kernel(x_ref, g_ref, wa_ref, wg_ref, ba_ref, bg_ref, u_ref, h_ref):
    @pl.when(pl.program_id(1) == 0)
    def _():
        x = x_ref[...]
        h_ref[...] = (x * _rms_scale(x) * g_ref[...]).astype(BF16)

    h = h_ref[...]
    a = jnp.dot(h, wa_ref[...].astype(BF16), preferred_element_type=F32) + ba_ref[...]
    gate = jnp.dot(h, wg_ref[...].astype(BF16), preferred_element_type=F32) + bg_ref[...]
    u_ref[...] = a * jax.nn.sigmoid(gate)


def _glu(x, g, w, b, *, tm, tn):
    m, d = x.shape
    nj = d // tn
    return pl.pallas_call(
        _glu_kernel,
        grid=(m // tm, nj),
        in_specs=[
            pl.BlockSpec((tm, d), lambda i, j: (i, 0)),
            pl.BlockSpec((1, d), lambda i, j: (0, 0)),
            pl.BlockSpec((d, tn), lambda i, j: (0, j)),
            pl.BlockSpec((d, tn), lambda i, j: (0, j + nj)),
            pl.BlockSpec((1, tn), lambda i, j: (0, j)),
            pl.BlockSpec((1, tn), lambda i, j: (0, j + nj)),
        ],
        out_specs=pl.BlockSpec((tm, tn), lambda i, j: (i, j)),
        out_shape=jax.ShapeDtypeStruct((m, d), F32),
        scratch_shapes=[pltpu.VMEM((tm, d), BF16)],
        compiler_params=_params("parallel", "arbitrary"),
        name="glu",
    )(x, g, w, w, b, b)


def _ln_silu_pw2(c, x, lng_ref, lnb_ref, w2_ref, b2_ref):
    mu = jnp.mean(c, axis=-1, keepdims=True)
    cc = c - mu
    var = jnp.mean(cc * cc, axis=-1, keepdims=True)
    y = cc * lax.rsqrt(var + EPS) * lng_ref[...] + lnb_ref[...]
    y = y * jax.nn.sigmoid(y)
    return x + jnp.dot(y.astype(BF16), w2_ref[...], preferred_element_type=F32) + b2_ref[...]


CONV_ROWS = 64
CONV_LANES = 128


def _conv_taps(win, wdw_ref, lanes):
    acc = jnp.zeros((CONV_ROWS, CONV_LANES), F32)
    for res in range(SUBLANES):
        shifted = win if res == 0 else pltpu.roll(win, CONV_ROWS + HALO - res, 0)
        for k in range(CONV_WIDTH):
            if (TAP0 + k) % SUBLANES == res:
                a = (TAP0 + k) // SUBLANES * SUBLANES
                acc = acc + shifted[a:a + CONV_ROWS] * wdw_ref[k:k + 1, lanes]
    return acc


def _glu_conv_kernel(x_ref, g_ref, w_ref, b_ref, wdw_ref, bdw_ref, c_ref, ulast_ref,
                     h_ref, win_a, win_b, hal_ref, *, tiles_per_seq):
    tm, d = x_ref.shape
    tn = win_a.shape[1]
    nj = d // tn
    x = x_ref[...]
    h_ref[...] = (x * _rms_scale(x) * g_ref[...]).astype(BF16)
    first = pl.program_id(0) % tiles_per_seq == 0

    @pl.when(pl.program_id(0) == 0)
    def _():
        hal_ref[...] = jnp.zeros_like(hal_ref)

    def glu(j, win_ref):
        cols = pl.ds(pl.multiple_of(j * tn, tn), tn)
        gcols = pl.ds(pl.multiple_of(d + j * tn, tn), tn)
        a = jnp.dot(h_ref[...], w_ref[:, cols], preferred_element_type=F32) + b_ref[:, cols]
        gate = jnp.dot(h_ref[...], w_ref[:, gcols], preferred_element_type=F32) + b_ref[:, gcols]
        u = a * jax.nn.sigmoid(gate)
        tail = u[tm - HALO:, :]
        win_ref[0:HALO, :] = jnp.where(first, 0.0, hal_ref[:, cols])
        win_ref[HALO:, :] = u
        hal_ref[:, cols] = tail
        ulast_ref[0, :, cols] = tail

    def conv(j, win_ref):
        for rc in range(tm // CONV_ROWS):
            for lc in range(tn // CONV_LANES):
                lanes = slice(lc * CONV_LANES, (lc + 1) * CONV_LANES)
                out_lanes = pl.ds(pl.multiple_of(j * tn + lc * CONV_LANES, CONV_LANES), CONV_LANES)
                win = win_ref[rc * CONV_ROWS:(rc + 1) * CONV_ROWS + HALO, lanes]
                c_ref[rc * CONV_ROWS:(rc + 1) * CONV_ROWS, out_lanes] = (
                    _conv_taps(win, wdw_ref, out_lanes) + bdw_ref[:, out_lanes])

    def tile_pair(p, carry):
        glu(2 * p + 1, win_b)
        conv(2 * p, win_a)
        glu(2 * p + 2, win_a)
        conv(2 * p + 1, win_b)
        return carry

    glu(0, win_a)
    lax.fori_loop(0, nj // 2 - 1, tile_pair, 0)
    glu(nj - 1, win_b)
    conv(nj - 2, win_a)
    conv(nj - 1, win_b)


def _glu_conv(x, g, w, b, wdw, bdw, *, seq_len, tm, tn):
    m, d = x.shape
    tiles_per_seq = seq_len // tm
    assert seq_len % tm == 0 and (d // tn) % 2 == 0
    return pl.pallas_call(
        functools.partial(_glu_conv_kernel, tiles_per_seq=tiles_per_seq),
        grid=(m // tm,),
        in_specs=[
            pl.BlockSpec((tm, d), lambda i: (i, 0)),
            _resident((1, d)),
            _resident((d, 2 * d)),
            _resident((1, 2 * d)),
            _resident((CONV_WIDTH, d)),
            _resident((1, d)),
        ],
        out_specs=[
            pl.BlockSpec((tm, d), lambda i: (i, 0)),
            pl.BlockSpec((1, HALO, d), lambda i: (i // tiles_per_seq, 0, 0)),
        ],
        out_shape=[jax.ShapeDtypeStruct((m, d), F32),
                   jax.ShapeDtypeStruct((m // seq_len, HALO, d), F32)],
        scratch_shapes=[pltpu.VMEM((tm, d), BF16), pltpu.VMEM((HALO + tm, tn), F32),
                        pltpu.VMEM((HALO + tm, tn), F32), pltpu.VMEM((HALO, d), F32)],
        compiler_params=_params("arbitrary"),
        name="glu_conv",
    )(x, g, w, b, wdw, bdw)


def _ln_pw2_kernel(c_ref, x_ref, lng_ref, lnb_ref, w2_ref, b2_ref, o_ref):
    o_ref[...] = _ln_silu_pw2(c_ref[...], x_ref[...], lng_ref, lnb_ref, w2_ref, b2_ref)


def _ln_pw2(c, x, lng, lnb, w2, b2, *, tm):
    m, d = x.shape
    rows = pl.BlockSpec((tm, d), lambda i: (i, 0))
    return pl.pallas_call(
        _ln_pw2_kernel,
        grid=(m // tm,),
        in_specs=[rows, rows, _resident((1, d)), _resident((1, d)), _resident((d, d)),
                  _resident((1, d))],
        out_specs=rows,
        out_shape=jax.ShapeDtypeStruct((m, d), F32),
        compiler_params=_params("parallel"),
        name="ln_pw2",
    )(c, x, lng, lnb, w2, b2)


def _conv_sample_kernel(full_ref, x_ref, wdw_ref, bdw_ref, lng_ref, lnb_ref, w2_ref, b2_ref,
                        o_ref, c_ref):
    nb, rows, d = full_ref.shape
    s = rows - (CONV_WIDTH - 1)
    for lc in range(d // CONV_LANES):
        lanes = slice(lc * CONV_LANES, (lc + 1) * CONV_LANES)
        acc = jnp.zeros((nb, s, CONV_LANES), F32)
        for k in range(CONV_WIDTH):
            acc = acc + full_ref[:, k:k + s, lanes] * wdw_ref[k:k + 1, lanes]
        c_ref[:, lanes] = acc.reshape(nb * s, CONV_LANES) + bdw_ref[:, lanes]
    o_ref[...] = _ln_silu_pw2(c_ref[...], x_ref[...], lng_ref, lnb_ref, w2_ref, b2_ref)


def _conv_sample(full, x, wdw, bdw, lng, lnb, w2, b2):
    m, d = x.shape
    whole = lambda a: pl.BlockSpec(a.shape, lambda i: (0,) * a.ndim)
    args = (full, x, wdw, bdw, lng, lnb, w2, b2)
    return pl.pallas_call(
        _conv_sample_kernel,
        grid=(1,),
        in_specs=[whole(a) for a in args],
        out_specs=pl.BlockSpec((m, d), lambda i: (0, 0)),
        out_shape=jax.ShapeDtypeStruct((m, d), F32),
        scratch_shapes=[pltpu.VMEM((m, d), F32)],
        compiler_params=_params("arbitrary"),
        name="conv_sample",
    )(*args)


def _mlp_kernel(x_ref, g_ref, wu_ref, wd_ref, fg_ref, o_ref, *rest, final, emit_weights):
    if emit_weights:
        wu_out_ref, wd_out_ref, h_ref = rest
    else:
        (h_ref,) = rest
    f = pl.program_id(1)

    @pl.when(f == 0)
    def _():
        x = x_ref[...]
        h_ref[...] = (x * _rms_scale(x) * g_ref[...]).astype(BF16)
        o_ref[...] = x

    wu = wu_ref[...].astype(BF16)
    wd = wd_ref[...].astype(BF16)
    if emit_weights:
        wu_out_ref[...] = wu
        wd_out_ref[...] = wd
    hid = jnp.dot(h_ref[...], wu, preferred_element_type=F32)
    hid = jnp.square(jnp.maximum(hid, 0.0)).astype(BF16)
    o_ref[...] += jnp.dot(hid, wd, preferred_element_type=F32)

    if final:
        @pl.when(f == pl.num_programs(1) - 1)
        def _():
            y = o_ref[...]
            o_ref[...] = y * _rms_scale(y) * fg_ref[...]


def _mlp(x, g, *, wu, wd, fg, final, tm, tf, layer=None):
    m, d = x.shape
    emit_weights = layer is not None
    if emit_weights:
        assert m == tm, "each weight tile must be visited exactly once"
        dff = wu.shape[2]
        w_specs = [pl.BlockSpec((None, d, tf), lambda i, f: (layer, 0, f)),
                   pl.BlockSpec((None, tf, d), lambda i, f: (layer, f, 0))]
    else:
        dff = wu.shape[1]
        w_specs = [pl.BlockSpec((d, tf), lambda i, f: (0, f)),
                   pl.BlockSpec((tf, d), lambda i, f: (f, 0))]
    out_specs = [pl.BlockSpec((tm, d), lambda i, f: (i, 0))]
    out_shape = [jax.ShapeDtypeStruct((m, d), F32)]
    if emit_weights:
        out_specs += [pl.BlockSpec((d, tf), lambda i, f: (0, f)),
                      pl.BlockSpec((tf, d), lambda i, f: (f, 0))]
        out_shape += [jax.ShapeDtypeStruct((d, dff), BF16), jax.ShapeDtypeStruct((dff, d), BF16)]
    out = pl.pallas_call(
        functools.partial(_mlp_kernel, final=final, emit_weights=emit_weights),
        grid=(m // tm, dff // tf),
        in_specs=[
            pl.BlockSpec((tm, d), lambda i, f: (i, 0)),
            pl.BlockSpec((1, d), lambda i, f: (0, 0)),
            *w_specs,
            pl.BlockSpec((1, d), lambda i, f: (0, 0)),
        ],
        out_specs=out_specs,
        out_shape=out_shape,
        scratch_shapes=[pltpu.VMEM((tm, d), BF16)],
        compiler_params=_params("parallel", "arbitrary"),
        name="mlp_final" if final else "mlp",
    )(x, g, wu, wd, fg)
    return out if emit_weights else out[0]


def _rope_tables(pos):
    half = ROT_DIM // 2
    inv = ROPE_THETA ** (-jnp.arange(half, dtype=F32) / half)
    ang = pos.astype(F32)[:, None] * inv[None, :]
    cos, sin = lax.optimization_barrier((jnp.cos(ang), jnp.sin(ang)))
    ones = jnp.ones((pos.shape[0], HEAD_DIM - ROT_DIM), F32)
    zeros = jnp.zeros_like(ones)
    zh = jnp.zeros_like(sin)
    per_head = lambda parts: jnp.tile(jnp.concatenate(parts, axis=1), (1, LANES // HEAD_DIM))
    return per_head([cos, cos, ones]), per_head([zh, sin, zeros]), per_head([-sin, zh, zeros])


def _rope_block(blk, cos, sin_up, sin_dn):
    half = ROT_DIM // 2
    return (blk * cos + pltpu.roll(blk, half, 1) * sin_up
            + pltpu.roll(blk, LANES - half, 1) * sin_dn)


PROJ_COLS = 1024


def _qkv_kernel(x_ref, gq_ref, gk_ref, w_ref, cos_ref, sup_ref, sdn_ref, q_ref, kv_ref):
    d = x_ref.shape[1]
    k_width = kv_ref.shape[1] // 2
    x = x_ref[...]
    xn = x * _rms_scale(x)
    hq = (xn * gq_ref[...]).astype(BF16)
    hk = (xn * gk_ref[...]).astype(BF16)
    tables = (cos_ref[...], sup_ref[...], sdn_ref[...])

    def project(h, w_col, out_ref, out_col, width, rope_cols):
        res = jnp.dot(h, w_ref[:, w_col:w_col + width], preferred_element_type=F32)
        for c in range(0, rope_cols, LANES):
            out_ref[:, out_col + c:out_col + c + LANES] = _rope_block(res[:, c:c + LANES], *tables)
        if rope_cols < width:
            out_ref[:, out_col + rope_cols:out_col + width] = res[:, rope_cols:]

    for c0 in range(0, d, PROJ_COLS):
        project(hq, c0, q_ref, c0, PROJ_COLS, PROJ_COLS)
    project(hk, d, kv_ref, 0, 2 * k_width, k_width)


def _qkv(x, gq, gk, w, tables, *, tm, table_blocks):
    m, d = x.shape
    n_kv = w.shape[1] - d
    tspec = pl.BlockSpec((tm, LANES), lambda i: (i % table_blocks, 0))
    return pl.pallas_call(
        _qkv_kernel,
        grid=(m // tm,),
        in_specs=[
            pl.BlockSpec((tm, d), lambda i: (i, 0)),
            _resident((1, d)),
            _resident((1, d)),
            _resident(w.shape),
            tspec, tspec, tspec,
        ],
        out_specs=[
            pl.BlockSpec((tm, d), lambda i: (i, 0)),
            pl.BlockSpec((tm, n_kv), lambda i: (i, 0)),
        ],
        out_shape=[jax.ShapeDtypeStruct((m, d), F32), jax.ShapeDtypeStruct((m, n_kv), F32)],
        compiler_params=_params("parallel"),
        name="qkv",
    )(x, gq, gk, w, *tables)


def _softmax_pv(s, sink, vv):
    m = jnp.maximum(jnp.max(s, axis=-1, keepdims=True), sink)
    p = jnp.exp(s - m)
    denom = jnp.sum(p, axis=-1, keepdims=True) + jnp.exp(sink - m)
    return jnp.dot(p.astype(BF16), vv, preferred_element_type=F32) * (1.0 / denom)


def _qk(q, k):
    return lax.dot_general(q, k, (((1,), (1,)), ((), ())), preferred_element_type=F32)


def _attn_prompt_kernel(q_ref, kvp_ref, kvc_ref, x_ref, wo_ref, sink_ref, o_ref, kv_ref, ot_ref):
    blk = kvp_ref.shape[1]
    n_blk = q_ref.shape[1] // blk
    k_width = kvc_ref.shape[2] // 2
    group = q_ref.shape[2] // k_width
    n_kv_heads = k_width // HEAD_DIM
    kv_ref[0:blk, :] = kvp_ref[0].astype(BF16)
    kv_ref[blk:, :] = kvc_ref[0].astype(BF16)

    key = lax.broadcasted_iota(jnp.int32, (2 * blk, group * blk), 0)
    qry = lax.broadcasted_iota(jnp.int32, (2 * blk, group * blk), 1) % blk
    own = (key >= blk) & (key - blk <= qry)
    bias_inner = jnp.where(((key < blk) & (key > qry)) | own, 0.0, -jnp.inf).astype(F32)
    bias_first = jnp.where(pl.program_id(1) > 0, bias_inner, jnp.where(own, 0.0, -jnp.inf))
    head_of_lane = lax.broadcasted_iota(jnp.int32, (1, group * blk), 1) // blk
    scale = HEAD_DIM ** -0.5

    def q_block(qb, carry):
        row0 = pl.multiple_of(qb * blk, blk)
        bias = jnp.where(qb == 0, bias_first, bias_inner)

        def scores(j):
            kk = kv_ref[pl.ds(row0, 2 * blk), j * HEAD_DIM:(j + 1) * HEAD_DIM]
            qs = jnp.concatenate(
                [q_ref[0, pl.ds(row0, blk), h * HEAD_DIM:(h + 1) * HEAD_DIM]
                 for h in range(j * group, (j + 1) * group)], axis=0)
            return _qk(kk, (qs * scale).astype(BF16)) + bias

        st = scores(0)
        for j in range(n_kv_heads):
            st_next = scores(j + 1) if j + 1 < n_kv_heads else None
            vv = kv_ref[pl.ds(row0, 2 * blk), k_width + j * HEAD_DIM:k_width + (j + 1) * HEAD_DIM]
            sink = jnp.zeros((1, group * blk), F32)
            for g in range(group):
                sink = jnp.where(head_of_lane == g, sink_ref[j * group + g], sink)
            m = jnp.maximum(jnp.max(st, axis=0, keepdims=True), sink)
            p = jnp.exp(st - m)
            denom = jnp.sum(p, axis=0, keepdims=True) + jnp.exp(sink - m)
            ot = lax.dot_general(vv, p.astype(BF16), (((0,), (0,)), ((), ())),
                                 preferred_element_type=F32) * (1.0 / denom)
            for g in range(group):
                h = j * group + g
                ot_ref[h * HEAD_DIM:(h + 1) * HEAD_DIM, pl.ds(row0, blk)] = (
                    ot[:, g * blk:(g + 1) * blk].astype(BF16))
            st = st_next
        return carry

    lax.fori_loop(0, n_blk, q_block, 0)

    proj = lax.dot_general(ot_ref[...], wo_ref[...], (((0,), (0,)), ((), ())),
                           preferred_element_type=F32)
    o_ref[0] = x_ref[0] + proj


def _attn_prompt(q, kv, x, wo, sinks, *, tq):
    b, t, d = x.shape
    n_kv = kv.shape[2]
    blk = WINDOW
    per = tq // blk
    return pl.pallas_call(
        _attn_prompt_kernel,
        grid=(b, t // tq),
        in_specs=[
            pl.BlockSpec((1, tq, d), lambda bi, n: (bi, n, 0)),
            pl.BlockSpec((1, blk, n_kv), lambda bi, n: (bi, jnp.maximum(n * per - 1, 0), 0)),
            pl.BlockSpec((1, tq, n_kv), lambda bi, n: (bi, n, 0)),
            pl.BlockSpec((1, tq, d), lambda bi, n: (bi, n, 0)),
            _resident((d, d)),
            pl.BlockSpec(memory_space=pltpu.SMEM),
        ],
        out_specs=pl.BlockSpec((1, tq, d), lambda bi, n: (bi, n, 0)),
        out_shape=jax.ShapeDtypeStruct((b, t, d), F32),
        scratch_shapes=[pltpu.VMEM((blk + tq, n_kv), BF16), pltpu.VMEM((d, tq), BF16)],
        compiler_params=_params("parallel", "arbitrary"),
        name="attn_prompt",
    )(q, kv, kv, x, wo, sinks)


def _attn_sample_kernel(q_ref, kvn_ref, ck_ref, cv_ref, x_ref, wo_ref, sink_ref,
                        o_ref, wk_ref, wv_ref, ocat_ref):
    gb, s, d = q_ref.shape
    win, k_width = ck_ref.shape[1], ck_ref.shape[2]
    group = d // k_width
    step = pl.program_id(0)

    wk_ref[:, 0:win - s, :] = ck_ref[:, s:win, :]
    wk_ref[:, win - s:win, :] = kvn_ref[:, :, 0:k_width]
    wv_ref[:, 0:win - s, :] = cv_ref[:, s:win, :]
    wv_ref[:, win - s:win, :] = kvn_ref[:, :, k_width:]

    qi = lax.broadcasted_iota(jnp.int32, (group * s, win + s), 0) % s
    kj = lax.broadcasted_iota(jnp.int32, (group * s, win + s), 1)
    rel = qi + win - kj
    bias = jnp.where((rel >= 0) & (rel < WINDOW), 0.0, -jnp.inf).astype(F32)
    head_of_row = lax.broadcasted_iota(jnp.int32, (group * s, 1), 0) // s
    scale = HEAD_DIM ** -0.5

    def one_batch(bb, carry):
        row0 = pl.multiple_of((step * gb + bb) * s, s)
        qb = q_ref[bb]
        kn = kvn_ref[bb]
        ck = ck_ref[bb]
        cv = cv_ref[bb]
        n_kv_heads = k_width // HEAD_DIM

        def scores(j):
            kl = slice(j * HEAD_DIM, (j + 1) * HEAD_DIM)
            kk = jnp.concatenate([ck[:, kl], kn[:, kl]], axis=0).astype(BF16)
            qs = jnp.concatenate([qb[:, h * HEAD_DIM:(h + 1) * HEAD_DIM]
                                  for h in range(j * group, (j + 1) * group)], axis=0)
            return _qk((qs * scale).astype(BF16), kk) + bias

        sc = scores(0)
        for j in range(n_kv_heads):
            sc_next = scores(j + 1) if j + 1 < n_kv_heads else None
            kl = slice(j * HEAD_DIM, (j + 1) * HEAD_DIM)
            vl = slice(k_width + j * HEAD_DIM, k_width + (j + 1) * HEAD_DIM)
            vv = jnp.concatenate([cv[:, kl], kn[:, vl]], axis=0).astype(BF16)
            heads = [j * group + g for g in range(group)]
            sink = jnp.zeros((group * s, 1), F32)
            for g, h in enumerate(heads):
                sink = jnp.where(head_of_row == g, sink_ref[h], sink)
            o = _softmax_pv(sc, sink, vv)
            for g, h in enumerate(heads):
                ocat_ref[pl.ds(row0, s), h * HEAD_DIM:(h + 1) * HEAD_DIM] = o[g * s:(g + 1) * s]
            sc = sc_next
        return carry

    lax.fori_loop(0, gb, one_batch, 0)

    @pl.when(step == pl.num_programs(0) - 1)
    def _():
        o_ref[...] = x_ref[...] + jnp.dot(ocat_ref[...].astype(BF16), wo_ref[...],
                                          preferred_element_type=F32)


def _attn_sample(q, kvn, ck, cv, x, wo, sinks, *, gb):
    nb, s, d = q.shape
    win, k_width = ck.shape[1], ck.shape[2]
    m = nb * s
    cache_spec = pl.BlockSpec((gb, win, k_width), lambda i: (i, 0, 0))
    return pl.pallas_call(
        _attn_sample_kernel,
        grid=(nb // gb,),
        in_specs=[
            pl.BlockSpec((gb, s, d), lambda i: (i, 0, 0)),
            pl.BlockSpec((gb, s, 2 * k_width), lambda i: (i, 0, 0)),
            cache_spec,
            cache_spec,
            _resident((m, d)),
            _resident((d, d)),
            pl.BlockSpec(memory_space=pltpu.SMEM),
        ],
        out_specs=[pl.BlockSpec((m, d), lambda i: (0, 0)), cache_spec, cache_spec],
        out_shape=[
            jax.ShapeDtypeStruct((m, d), F32),
            jax.ShapeDtypeStruct(ck.shape, F32),
            jax.ShapeDtypeStruct(cv.shape, F32),
        ],
        scratch_shapes=[pltpu.VMEM((m, d), F32)],
        compiler_params=_params("arbitrary"),
        name="attn_sample",
    )(q, kvn, ck, cv, x, wo, sinks)


def _row(v):
    return v.reshape(1, -1)


def _as_list(out):
    return list(out) if isinstance(out, (list, tuple)) else [out]


def _tile(m, want):
    return want if m % want == 0 else m


def kernel(x_prompt, x_sample, state_conv, cache_k, cache_v, norm_mix, w_pw1, b_pw1, w_dw, b_dw,
           conv_ln_g, conv_ln_b, w_pw2, b_pw2, kv_norm, w_k, w_v, w_q, w_o, sinks, norm_mlp,
           w_up, w_down, final_norm):
    b, t, d = x_prompt.shape
    nb, s, _ = x_sample.shape
    win = cache_k.shape[1]
    k_width = cache_k.shape[2] * cache_k.shape[3]
    depth = norm_mlp.shape[0]
    n_conv = w_pw1.shape[0]
    assert depth == 2 and n_conv == 1 and w_q.shape[0] == 1, "one conv layer then one attention layer"
    assert t % WINDOW == 0 and win == WINDOW

    w_pw1b = w_pw1[0].astype(BF16)
    w_pw2b = w_pw2[0].astype(BF16)
    w_ob = w_o[0].astype(BF16)
    w_qkvb = jnp.concatenate([w_q[0], w_k, w_v], axis=1).astype(BF16)
    ln_pw2_w = (_row(conv_ln_g[0]), _row(conv_ln_b[0]), w_pw2b, _row(b_pw2[0]))

    def tokens(x, *, mix, attend, tables, table_blocks, tm_qkv, mlp_weights):
        m = x.shape[0]
        tm = _tile(m, 1024)
        fg = _row(final_norm)
        x, conv_state = mix(x)
        x, *w0 = _as_list(_mlp(x, _row(norm_mlp[0]), fg=fg, final=False, tm=tm, tf=512,
                               **mlp_weights(0)))
        q, kv = _qkv(x, _row(norm_mix[1]), _row(kv_norm), w_qkvb, tables, tm=tm_qkv,
                     table_blocks=table_blocks)
        x, extra = attend(q, kv, x)
        y, *w1 = _as_list(_mlp(x, _row(norm_mlp[1]), fg=fg, final=True, tm=tm, tf=512,
                               **mlp_weights(1)))
        return y, conv_state, kv, extra, (w0, w1)

    def mix_p(x):
        c, u_last = _glu_conv(x, _row(norm_mix[0]), w_pw1b, _row(b_pw1[0]), w_dw[0], _row(b_dw[0]),
                              seq_len=t, tm=512, tn=256)
        return _ln_pw2(c, x, *ln_pw2_w, tm=512), u_last[:, HALO - (CONV_WIDTH - 1):]

    def attend_p(q, kv, x):
        x3 = _attn_prompt(q.reshape(b, t, d), kv.reshape(b, t, -1), x.reshape(b, t, d), w_ob,
                          sinks[0], tq=512)
        return x3.reshape(b * t, d), None

    pos_s = PAST_LEN + jnp.arange(s, dtype=jnp.int32)

    def mix_s(x):
        u = _glu(x, _row(norm_mix[0]), w_pw1b, _row(b_pw1[0]), tm=nb * s, tn=512)
        full = jnp.concatenate([state_conv[0], u.reshape(nb, s, d)], axis=1)
        x = _conv_sample(full, x, w_dw[0], _row(b_dw[0]), *ln_pw2_w)
        return x, full[:, -(CONV_WIDTH - 1):]

    def attend_s(q, kv, x):
        x, wk, wv = _attn_sample(q.reshape(nb, s, d), kv.reshape(nb, s, -1),
                                 cache_k.reshape(nb, win, k_width),
                                 cache_v.reshape(nb, win, k_width), x, w_ob, sinks[0], gb=8)
        return x, (wk.reshape(cache_k.shape), wv.reshape(cache_v.shape))

    tables_s = tuple(jnp.tile(tb, (nb, 1)) for tb in _rope_tables(pos_s))
    y_s, conv_s_state, _, (win_k_s, win_v_s), mlp_b = tokens(
        x_sample.reshape(nb * s, d), mix=mix_s, attend=attend_s, tables=tables_s,
        table_blocks=1, tm_qkv=nb * s, mlp_weights=lambda l: dict(wu=w_up, wd=w_down, layer=l))

    tm_p = _tile(b * t, 512)
    y_p, conv_p_state, kv_p, _, _ = tokens(
        x_prompt.reshape(b * t, d), mix=mix_p, attend=attend_p,
        tables=_rope_tables(jnp.arange(t, dtype=jnp.int32)), table_blocks=t // tm_p, tm_qkv=tm_p,
        mlp_weights=lambda l: dict(wu=mlp_b[l][0], wd=mlp_b[l][1]))
    kv_p = kv_p.reshape(b, t, -1)[:, t - WINDOW:].reshape(b, WINDOW, 2, N_KV_HEADS, HEAD_DIM)

    return (y_p.reshape(b, t, d), y_s.reshape(nb, s, d), conv_p_state[None], conv_s_state[None],
            kv_p[:, :, 0], kv_p[:, :, 1], win_k_s, win_v_s)
```

```python
import functools

import jax
import jax.numpy as jnp
from jax import lax
from jax.experimental import pallas as pl
from jax.experimental.pallas import tpu as pltpu

F32 = jnp.float32
BF16 = jnp.bfloat16

EPS = 1e-6
HEAD_DIM = 64
N_KV_HEADS = 8
ROT_DIM = HEAD_DIM // 4
ROPE_THETA = 500000.0
WINDOW = 128
CONV_WIDTH = 31
PAST_LEN = 16384

LANES = 128
SUBLANES = 8
VMEM_LIMIT_BYTES = 54 * 1024 * 1024

HALO = -(-(CONV_WIDTH - 1) // SUBLANES) * SUBLANES
TAP0 = HALO - (CONV_WIDTH - 1)


def _params(*semantics):
    return pltpu.CompilerParams(dimension_semantics=semantics, vmem_limit_bytes=VMEM_LIMIT_BYTES)


def _resident(shape):
    return pl.BlockSpec(shape, lambda *_: (0,) * len(shape), pipeline_mode=pl.Buffered(1))


def _rms_scale(x):
    return lax.rsqrt(jnp.mean(x * x, axis=-1, keepdims=True) + EPS)


def _glu_kernel(x_ref, g_ref, wa_ref, wg_ref, ba_ref, bg_ref, u_ref, h_ref):
    @pl.when(pl.program_id(1) == 0)
    def _():
        x = x_ref[...]
        h_ref[...] = (x * _rms_scale(x) * g_ref[...]).astype(BF16)

    h = h_ref[...]
    a = jnp.dot(h, wa_ref[...].astype(BF16), preferred_element_type=F32) + ba_ref[...]
    gate = jnp.dot(h, wg_ref[...].astype(BF16), preferred_element_type=F32) + bg_ref[...]
    u_ref[...] = a * jax.nn.sigmoid(gate)


def _glu(x, g, w, b, *, tm, tn):
    m, d = x.shape
    nj = d // tn
    return pl.pallas_call(
        _glu_kernel,
        grid=(m // tm, nj),
        in_specs=[
            pl.BlockSpec((tm, d), lambda i, j: (i, 0)),
            pl.BlockSpec((1, d), lambda i, j: (0, 0)),
            pl.BlockSpec((d, tn), lambda i, j: (0, j)),
            pl.BlockSpec((d, tn), lambda i, j: (0, j + nj)),
            pl.BlockSpec((1, tn), lambda i, j: (0, j)),
            pl.BlockSpec((1, tn), lambda i, j: (0, j + nj)),
        ],
        out_specs=pl.BlockSpec((tm, tn), lambda i, j: (i, j)),
        out_shape=jax.ShapeDtypeStruct((m, d), F32),
        scratch_shapes=[pltpu.VMEM((tm, d), BF16)],
        compiler_params=_params("parallel", "arbitrary"),
        name="glu",
    )(x, g, w, w, b, b)


def _ln_silu_pw2(c, x, lng_ref, lnb_ref, w2_ref, b2_ref):
    mu = jnp.mean(c, axis=-1, keepdims=True)
    cc = c - mu
    var = jnp.mean(cc * cc, axis=-1, keepdims=True)
    y = cc * lax.rsqrt(var + EPS) * lng_ref[...] + lnb_ref[...]
    y = y * jax.nn.sigmoid(y)
    return x + jnp.dot(y.astype(BF16), w2_ref[...], preferred_element_type=F32) + b2_ref[...]


CONV_ROWS = 64
CONV_LANES = 128


def _conv_taps(win, wdw_ref, lanes):
    acc = jnp.zeros((CONV_ROWS, CONV_LANES), F32)
    for res in range(SUBLANES):
        shifted = win if res == 0 else pltpu.roll(win, CONV_ROWS + HALO - res, 0)
        for k in range(CONV_WIDTH):
            if (TAP0 + k) % SUBLANES == res:
                a = (TAP0 + k) // SUBLANES * SUBLANES
                acc = acc + shifted[a:a + CONV_ROWS] * wdw_ref[k:k + 1, lanes]
    return acc


def _glu_conv_kernel(x_ref, g_ref, w_ref, b_ref, wdw_ref, bdw_ref, c_ref, ulast_ref,
                     h_ref, win_a, win_b, hal_ref, *, tiles_per_seq):
    tm, d = x_ref.shape
    tn = win_a.shape[1]
    nj = d // tn
    x = x_ref[...]
    h_ref[...] = (x * _rms_scale(x) * g_ref[...]).astype(BF16)
    first = pl.program_id(0) % tiles_per_seq == 0

    @pl.when(pl.program_id(0) == 0)
    def _():
        hal_ref[...] = jnp.zeros_like(hal_ref)

    def glu(j, win_ref):
        cols = pl.ds(pl.multiple_of(j * tn, tn), tn)
        gcols = pl.ds(pl.multiple_of(d + j * tn, tn), tn)
        a = jnp.dot(h_ref[...], w_ref[:, cols], preferred_element_type=F32) + b_ref[:, cols]
        gate = jnp.dot(h_ref[...], w_ref[:, gcols], preferred_element_type=F32) + b_ref[:, gcols]
        u = a * jax.nn.sigmoid(gate)
        tail = u[tm - HALO:, :]
        win_ref[0:HALO, :] = jnp.where(first, 0.0, hal_ref[:, cols])
        win_ref[HALO:, :] = u
        hal_ref[:, cols] = tail
        ulast_ref[0, :, cols] = tail

    def conv(j, win_ref):
        for rc in range(tm // CONV_ROWS):
            for lc in range(tn // CONV_LANES):
                lanes = slice(lc * CONV_LANES, (lc + 1) * CONV_LANES)
                out_lanes = pl.ds(pl.multiple_of(j * tn + lc * CONV_LANES, CONV_LANES), CONV_LANES)
                win = win_ref[rc * CONV_ROWS:(rc + 1) * CONV_ROWS + HALO, lanes]
                c_ref[rc * CONV_ROWS:(rc + 1) * CONV_ROWS, out_lanes] = (
                    _conv_taps(win, wdw_ref, out_lanes) + bdw_ref[:, out_lanes])

    def tile_pair(p, carry):
        glu(2 * p + 1, win_b)
        conv(2 * p, win_a)
        glu(2 * p + 2, win_a)
        conv(2 * p + 1, win_b)
        return carry

    glu(0, win_a)
    lax.fori_loop(0, nj // 2 - 1, tile_pair, 0)
    glu(nj - 1, win_b)
    conv(nj - 2, win_a)
    conv(nj - 1, win_b)


def _glu_conv(x, g, w, b, wdw, bdw, *, seq_len, tm, tn):
    m, d = x.shape
    tiles_per_seq = seq_len // tm
    assert seq_len % tm == 0 and (d // tn) % 2 == 0
    return pl.pallas_call(
        functools.partial(_glu_conv_kernel, tiles_per_seq=tiles_per_seq),
        grid=(m // tm,),
        in_specs=[
            pl.BlockSpec((tm, d), lambda i: (i, 0)),
            _resident((1, d)),
            _resident((d, 2 * d)),
            _resident((1, 2 * d)),
            _resident((CONV_WIDTH, d)),
            _resident((1, d)),
        ],
        out_specs=[
            pl.BlockSpec((tm, d), lambda i: (i, 0)),
            pl.BlockSpec((1, HALO, d), lambda i: (i // tiles_per_seq, 0, 0)),
        ],
        out_shape=[jax.ShapeDtypeStruct((m, d), F32),
                   jax.ShapeDtypeStruct((m // seq_len, HALO, d), F32)],
        scratch_shapes=[pltpu.VMEM((tm, d), BF16), pltpu.VMEM((HALO + tm, tn), F32),
                        pltpu.VMEM((HALO + tm, tn), F32), pltpu.VMEM((HALO, d), F32)],
        compiler_params=_params("arbitrary"),
        name="glu_conv",
    )(x, g, w, b, wdw, bdw)


def _ln_pw2_kernel(c_ref, x_ref, lng_ref, lnb_ref, w2_ref, b2_ref, o_ref):
    o_ref[...] = _ln_silu_pw2(c_ref[...], x_ref[...], lng_ref, lnb_ref, w2_ref, b2_ref)


def _ln_pw2(c, x, lng, lnb, w2, b2, *, tm):
    m, d = x.shape
    rows = pl.BlockSpec((tm, d), lambda i: (i, 0))
    return pl.pallas_call(
        _ln_pw2_kernel,
        grid=(m // tm,),
        in_specs=[rows, rows, _resident((1, d)), _resident((1, d)), _resident((d, d)),
                  _resident((1, d))],
        out_specs=rows,
        out_shape=jax.ShapeDtypeStruct((m, d), F32),
        compiler_params=_params("parallel"),
        name="ln_pw2",
    )(c, x, lng, lnb, w2, b2)


def _conv_sample_kernel(full_ref, x_ref, wdw_ref, bdw_ref, lng_ref, lnb_ref, w2_ref, b2_ref,
                        o_ref, c_ref):
    nb, rows, d = full_ref.shape
    s = rows - (CONV_WIDTH - 1)
    for lc in range(d // CONV_LANES):
        lanes = slice(lc * CONV_LANES, (lc + 1) * CONV_LANES)
        acc = jnp.zeros((nb, s, CONV_LANES), F32)
        for k in range(CONV_WIDTH):
            acc = acc + full_ref[:, k:k + s, lanes] * wdw_ref[k:k + 1, lanes]
        c_ref[:, lanes] = acc.reshape(nb * s, CONV_LANES) + bdw_ref[:, lanes]
    o_ref[...] = _ln_silu_pw2(c_ref[...], x_ref[...], lng_ref, lnb_ref, w2_ref, b2_ref)


def _conv_sample(full, x, wdw, bdw, lng, lnb, w2, b2):
    m, d = x.shape
    whole = lambda a: pl.BlockSpec(a.shape, lambda i: (0,) * a.ndim)
    args = (full, x, wdw, bdw, lng, lnb, w2, b2)
    return pl.pallas_call(
        _conv_sample_kernel,
        grid=(1,),
        in_specs=[whole(a) for a in args],
        out_specs=pl.BlockSpec((m, d), lambda i: (0, 0)),
        out_shape=jax.ShapeDtypeStruct((m, d), F32),
        scratch_shapes=[pltpu.VMEM((m, d), F32)],
        compiler_params=_params("arbitrary"),
        name="conv_sample",
    )(*args)


def _mlp_kernel(x_ref, g_ref, wu_ref, wd_ref, fg_ref, o_ref, *rest, final, emit_weights):
    if emit_weights:
        wu_out_ref, wd_out_ref, h_ref = rest
    else:
        (h_ref,) = rest
    f = pl.program_id(1)

    @pl.when(f == 0)
    def _():
        x = x_ref[...]
        h_ref[...] = (x * _rms_scale(x) * g_ref[...]).astype(BF16)
        o_ref[...] = x

    wu = wu_ref[...].astype(BF16)
    wd = wd_ref[...].astype(BF16)
    if emit_weights:
        wu_out_ref[...] = wu
        wd_out_ref[...] = wd
    hid = jnp.dot(h_ref[...], wu, preferred_element_type=F32)
    hid = jnp.square(jnp.maximum(hid, 0.0)).astype(BF16)
    o_ref[...] += jnp.dot(hid, wd, preferred_element_type=F32)

    if final:
        @pl.when(f == pl.num_programs(1) - 1)
        def _():
            y = o_ref[...]
            o_ref[...] = y * _rms_scale(y) * fg_ref[...]


def _mlp(x, g, *, wu, wd, fg, final, tm, tf, layer=None):
    m, d = x.shape
    emit_weights = layer is not None
    if emit_weights:
        assert m == tm, "each weight tile must be visited exactly once"
        dff = wu.shape[2]
        w_specs = [pl.BlockSpec((None, d, tf), lambda i, f: (layer, 0, f)),
                   pl.BlockSpec((None, tf, d), lambda i, f: (layer, f, 0))]
    else:
        dff = wu.shape[1]
        w_specs = [pl.BlockSpec((d, tf), lambda i, f: (0, f)),
                   pl.BlockSpec((tf, d), lambda i, f: (f, 0))]
    out_specs = [pl.BlockSpec((tm, d), lambda i, f: (i, 0))]
    out_shape = [jax.ShapeDtypeStruct((m, d), F32)]
    if emit_weights:
        out_specs += [pl.BlockSpec((d, tf), lambda i, f: (0, f)),
                      pl.BlockSpec((tf, d), lambda i, f: (f, 0))]
        out_shape += [jax.ShapeDtypeStruct((d, dff), BF16), jax.ShapeDtypeStruct((dff, d), BF16)]
    out = pl.pallas_call(
        functools.partial(_mlp_kernel, final=final, emit_weights=emit_weights),
        grid=(m // tm, dff // tf),
        in_specs=[
            pl.BlockSpec((tm, d), lambda i, f: (i, 0)),
            pl.BlockSpec((1, d), lambda i, f: (0, 0)),
            *w_specs,
            pl.BlockSpec((1, d), lambda i, f: (0, 0)),
        ],
        out_specs=out_specs,
        out_shape=out_shape,
        scratch_shapes=[pltpu.VMEM((tm, d), BF16)],
        compiler_params=_params("parallel", "arbitrary"),
        name="mlp_final" if final else "mlp",
    )(x, g, wu, wd, fg)
    return out if emit_weights else out[0]


def _rope_tables(pos):
    half = ROT_DIM // 2
    inv = ROPE_THETA ** (-jnp.arange(half, dtype=F32) / half)
    ang = pos.astype(F32)[:, None] * inv[None, :]
    cos, sin = lax.optimization_barrier((jnp.cos(ang), jnp.sin(ang)))
    ones = jnp.ones((pos.shape[0], HEAD_DIM - ROT_DIM), F32)
    zeros = jnp.zeros_like(ones)
    zh = jnp.zeros_like(sin)
    per_head = lambda parts: jnp.tile(jnp.concatenate(parts, axis=1), (1, LANES // HEAD_DIM))
    return per_head([cos, cos, ones]), per_head([zh, sin, zeros]), per_head([-sin, zh, zeros])


def _rope_block(blk, cos, sin_up, sin_dn):
    half = ROT_DIM // 2
    return (blk * cos + pltpu.roll(blk, half, 1) * sin_up
            + pltpu.roll(blk, LANES - half, 1) * sin_dn)


PROJ_COLS = 1024


def _qkv_kernel(x_ref, gq_ref, gk_ref, w_ref, cos_ref, sup_ref, sdn_ref, q_ref, kv_ref):
    d = x_ref.shape[1]
    k_width = kv_ref.shape[1] // 2
    x = x_ref[...]
    xn = x * _rms_scale(x)
    hq = (xn * gq_ref[...]).astype(BF16)
    hk = (xn * gk_ref[...]).astype(BF16)
    tables = (cos_ref[...], sup_ref[...], sdn_ref[...])

    def project(h, w_col, out_ref, out_col, width, rope_cols):
        res = jnp.dot(h, w_ref[:, w_col:w_col + width], preferred_element_type=F32)
        for c in range(0, rope_cols, LANES):
            out_ref[:, out_col + c:out_col + c + LANES] = _rope_block(res[:, c:c + LANES], *tables)
        if rope_cols < width:
            out_ref[:, out_col + rope_cols:out_col + width] = res[:, rope_cols:]

    for c0 in range(0, d, PROJ_COLS):
        project(hq, c0, q_ref, c0, PROJ_COLS, PROJ_COLS)
    project(hk, d, kv_ref, 0, 2 * k_width, k_width)


def _qkv(x, gq, gk, w, tables, *, tm, table_blocks):
    m, d = x.shape
    n_kv = w.shape[1] - d
    tspec = pl.BlockSpec((tm, LANES), lambda i: (i % table_blocks, 0))
    return pl.pallas_call(
        _qkv_kernel,
        grid=(m // tm,),
        in_specs=[
            pl.BlockSpec((tm, d), lambda i: (i, 0)),
            _resident((1, d)),
            _resident((1, d)),
            _resident(w.shape),
            tspec, tspec, tspec,
        ],
        out_specs=[
            pl.BlockSpec((tm, d), lambda i: (i, 0)),
            pl.BlockSpec((tm, n_kv), lambda i: (i, 0)),
        ],
        out_shape=[jax.ShapeDtypeStruct((m, d), F32), jax.ShapeDtypeStruct((m, n_kv), F32)],
        compiler_params=_params("parallel"),
        name="qkv",
    )(x, gq, gk, w, *tables)


def _qk(q, k):
    return lax.dot_general(q, k, (((1,), (1,)), ((), ())), preferred_element_type=F32)


def _attn_prompt_kernel(q_ref, kvp_ref, kvc_ref, x_ref, wo_ref, sink_ref, o_ref, kv_ref, ot_ref):
    blk = kvp_ref.shape[1]
    n_blk = q_ref.shape[1] // blk
    k_width = kvc_ref.shape[2] // 2
    group = q_ref.shape[2] // k_width
    n_kv_heads = k_width // HEAD_DIM
    kv_ref[0:blk, :] = kvp_ref[0].astype(BF16)
    kv_ref[blk:, :] = kvc_ref[0].astype(BF16)

    key = lax.broadcasted_iota(jnp.int32, (2 * blk, group * blk), 0)
    qry = lax.broadcasted_iota(jnp.int32, (2 * blk, group * blk), 1) % blk
    own = (key >= blk) & (key - blk <= qry)
    bias_inner = jnp.where(((key < blk) & (key > qry)) | own, 0.0, -jnp.inf).astype(F32)
    bias_first = jnp.where(pl.program_id(1) > 0, bias_inner, jnp.where(own, 0.0, -jnp.inf))
    head_of_lane = lax.broadcasted_iota(jnp.int32, (1, group * blk), 1) // blk
    scale = HEAD_DIM ** -0.5

    def q_block(qb, carry):
        row0 = pl.multiple_of(qb * blk, blk)
        bias = jnp.where(qb == 0, bias_first, bias_inner)

        def scores(j):
            kk = kv_ref[pl.ds(row0, 2 * blk), j * HEAD_DIM:(j + 1) * HEAD_DIM]
            qs = jnp.concatenate(
                [q_ref[0, pl.ds(row0, blk), h * HEAD_DIM:(h + 1) * HEAD_DIM]
                 for h in range(j * group, (j + 1) * group)], axis=0)
            return _qk(kk, (qs * scale).astype(BF16)) + bias

        st = scores(0)
        for j in range(n_kv_heads):
            st_next = scores(j + 1) if j + 1 < n_kv_heads else None
            vv = kv_ref[pl.ds(row0, 2 * blk), k_width + j * HEAD_DIM:k_width + (j + 1) * HEAD_DIM]
            sink = jnp.zeros((1, group * blk), F32)
            for g in range(group):
                sink = jnp.where(head_of_lane == g, sink_ref[j * group + g], sink)
            m = jnp.maximum(jnp.max(st, axis=0, keepdims=True), sink)
            p = jnp.exp(st - m)
            denom = jnp.sum(p, axis=0, keepdims=True) + jnp.exp(sink - m)
            ot = lax.dot_general(vv, p.astype(BF16), (((0,), (0,)), ((), ())),
                                 preferred_element_type=F32) * (1.0 / denom)
            for g in range(group):
                h = j * group + g
                ot_ref[h * HEAD_DIM:(h + 1) * HEAD_DIM, pl.ds(row0, blk)] = (
                    ot[:, g * blk:(g + 1) * blk].astype(BF16))
            st = st_next
        return carry

    lax.fori_loop(0, n_blk, q_block, 0)

    proj = lax.dot_general(ot_ref[...], wo_ref[...], (((0,), (0,)), ((), ())),
                           preferred_element_type=F32)
    o_ref[0] = x_ref[0] + proj


def _attn_prompt(q, kv, x, wo, sinks, *, tq):
    b, t, d = x.shape
    n_kv = kv.shape[2]
    blk = WINDOW
    per = tq // blk
    return pl.pallas_call(
        _attn_prompt_kernel,
        grid=(b, t // tq),
        in_specs=[
            pl.BlockSpec((1, tq, d), lambda bi, n: (bi, n, 0)),
            pl.BlockSpec((1, blk, n_kv), lambda bi, n: (bi, jnp.maximum(n * per - 1, 0), 0)),
            pl.BlockSpec((1, tq, n_kv), lambda bi, n: (bi, n, 0)),
            pl.BlockSpec((1, tq, d), lambda bi, n: (bi, n, 0)),
            _resident((d, d)),
            pl.BlockSpec(memory_space=pltpu.SMEM),
        ],
        out_specs=pl.BlockSpec((1, tq, d), lambda bi, n: (bi, n, 0)),
        out_shape=jax.ShapeDtypeStruct((b, t, d), F32),
        scratch_shapes=[pltpu.VMEM((blk + tq, n_kv), BF16), pltpu.VMEM((d, tq), BF16)],
        compiler_params=_params("parallel", "arbitrary"),
        name="attn_prompt",
    )(q, kv, kv, x, wo, sinks)


def _attn_sample_kernel(q_ref, kvn_ref, ck_ref, cv_ref, x_ref, wo_ref, sink_ref,
                        o_ref, wk_ref, wv_ref, ocat_ref):
    gb, s, d = q_ref.shape
    _, n_kv_heads, hd, win = ck_ref.shape
    k_width = n_kv_heads * hd
    group = d // k_width
    step = pl.program_id(0)

    lane = lax.broadcasted_iota(jnp.int32, (1, win), 1)
    is_new = lane >= win - s
    qi = lax.broadcasted_iota(jnp.int32, (group * s, win), 0) % s
    kl = lax.broadcasted_iota(jnp.int32, (group * s, win), 1)
    bias_new = jnp.where(kl + s <= qi + win, 0.0, -jnp.inf).astype(F32)
    bias_old = jnp.where((kl < s) & (kl > qi), 0.0, -jnp.inf).astype(F32)
    head_of_row = lax.broadcasted_iota(jnp.int32, (group * s, 1), 0) // s
    scale = hd ** -0.5
    nt = (((1,), (1,)), ((), ()))

    def one_batch(bb, carry):
        row0 = pl.multiple_of((step * gb + bb) * s, s)
        qb = q_ref[bb]
        kn = kvn_ref[bb]
        pad = jnp.zeros((win - s, k_width), F32)
        kn_t = jnp.concatenate([pad, kn[:, :k_width]], axis=0).T
        vn_t = jnp.concatenate([pad, kn[:, k_width:]], axis=0).T

        def window(old, new_t, j):
            shifted = pltpu.roll(old, win - s, 1)
            return jnp.where(is_new, new_t[j * hd:(j + 1) * hd, :], shifted)

        def scores(j):
            k_old = ck_ref[bb, j]
            k_win = window(k_old, kn_t, j)
            wk_ref[bb, j] = k_win
            qs = jnp.concatenate([qb[:, h * hd:(h + 1) * hd]
                                  for h in range(j * group, (j + 1) * group)], axis=0)
            qs = (qs * scale).astype(BF16)
            s_new = jnp.dot(qs, k_win.astype(BF16), preferred_element_type=F32) + bias_new
            s_old = jnp.dot(qs, k_old.astype(BF16), preferred_element_type=F32) + bias_old
            return s_new, s_old

        sc = scores(0)
        for j in range(n_kv_heads):
            sc_next = scores(j + 1) if j + 1 < n_kv_heads else None
            v_old = cv_ref[bb, j]
            v_win = window(v_old, vn_t, j)
            wv_ref[bb, j] = v_win
            sink = jnp.zeros((group * s, 1), F32)
            for g in range(group):
                sink = jnp.where(head_of_row == g, sink_ref[j * group + g], sink)
            s_new, s_old = sc
            m = jnp.maximum(jnp.maximum(jnp.max(s_new, axis=-1, keepdims=True),
                                        jnp.max(s_old, axis=-1, keepdims=True)), sink)
            p_new = jnp.exp(s_new - m)
            p_old = jnp.exp(s_old - m)
            denom = (jnp.sum(p_new, axis=-1, keepdims=True) + jnp.sum(p_old, axis=-1, keepdims=True)
                     + jnp.exp(sink - m))
            o = (lax.dot_general(p_new.astype(BF16), v_win.astype(BF16), nt,
                                 preferred_element_type=F32)
                 + lax.dot_general(p_old.astype(BF16), v_old.astype(BF16), nt,
                                   preferred_element_type=F32)) * (1.0 / denom)
            for g in range(group):
                h = j * group + g
                ocat_ref[pl.ds(row0, s), h * hd:(h + 1) * hd] = o[g * s:(g + 1) * s]
            sc = sc_next
        return carry

    lax.fori_loop(0, gb, one_batch, 0)

    @pl.when(step == pl.num_programs(0) - 1)
    def _():
        o_ref[...] = x_ref[...] + jnp.dot(ocat_ref[...].astype(BF16), wo_ref[...],
                                          preferred_element_type=F32)


def _attn_sample(q, kvn, ck, cv, x, wo, sinks, *, gb):
    nb, s, d = q.shape
    m = nb * s
    to_lanes, from_lanes = (0, 2, 3, 1), (0, 3, 1, 2)
    ck_t, cv_t = ck.transpose(to_lanes), cv.transpose(to_lanes)
    cache_spec = pl.BlockSpec((gb,) + ck_t.shape[1:], lambda i: (i, 0, 0, 0))
    x, wk_t, wv_t = pl.pallas_call(
        _attn_sample_kernel,
        grid=(nb // gb,),
        in_specs=[
            pl.BlockSpec((gb, s, d), lambda i: (i, 0, 0)),
            pl.BlockSpec((gb, s, kvn.shape[2]), lambda i: (i, 0, 0)),
            cache_spec,
            cache_spec,
            _resident((m, d)),
            _resident((d, d)),
            pl.BlockSpec(memory_space=pltpu.SMEM),
        ],
        out_specs=[pl.BlockSpec((m, d), lambda i: (0, 0)), cache_spec, cache_spec],
        out_shape=[
            jax.ShapeDtypeStruct((m, d), F32),
            jax.ShapeDtypeStruct(ck_t.shape, F32),
            jax.ShapeDtypeStruct(cv_t.shape, F32),
        ],
        scratch_shapes=[pltpu.VMEM((m, d), F32)],
        compiler_params=_params("arbitrary"),
        name="attn_sample",
    )(q, kvn, ck_t, cv_t, x, wo, sinks)
    return x, wk_t.transpose(from_lanes), wv_t.transpose(from_lanes)


def _row(v):
    return v.reshape(1, -1)


def _as_list(out):
    return list(out) if isinstance(out, (list, tuple)) else [out]


def _tile(m, want):
    return want if m % want == 0 else m


def kernel(x_prompt, x_sample, state_conv, cache_k, cache_v, norm_mix, w_pw1, b_pw1, w_dw, b_dw,
           conv_ln_g, conv_ln_b, w_pw2, b_pw2, kv_norm, w_k, w_v, w_q, w_o, sinks, norm_mlp,
           w_up, w_down, final_norm):
    b, t, d = x_prompt.shape
    nb, s, _ = x_sample.shape
    win = cache_k.shape[1]
    depth = norm_mlp.shape[0]
    n_conv = w_pw1.shape[0]
    assert depth == 2 and n_conv == 1 and w_q.shape[0] == 1, "one conv layer then one attention layer"
    assert t % WINDOW == 0 and win == WINDOW

    w_pw1b = w_pw1[0].astype(BF16)
    w_pw2b = w_pw2[0].astype(BF16)
    w_ob = w_o[0].astype(BF16)
    w_qkvb = jnp.concatenate([w_q[0], w_k, w_v], axis=1).astype(BF16)
    ln_pw2_w = (_row(conv_ln_g[0]), _row(conv_ln_b[0]), w_pw2b, _row(b_pw2[0]))

    def tokens(x, *, mix, attend, tables, table_blocks, tm_qkv, mlp_weights):
        m = x.shape[0]
        tm = _tile(m, 1024)
        fg = _row(final_norm)
        x, conv_state = mix(x)
        x, *w0 = _as_list(_mlp(x, _row(norm_mlp[0]), fg=fg, final=False, tm=tm, tf=512,
                               **mlp_weights(0)))
        q, kv = _qkv(x, _row(norm_mix[1]), _row(kv_norm), w_qkvb, tables, tm=tm_qkv,
                     table_blocks=table_blocks)
        x, extra = attend(q, kv, x)
        y, *w1 = _as_list(_mlp(x, _row(norm_mlp[1]), fg=fg, final=True, tm=tm, tf=512,
                               **mlp_weights(1)))
        return y, conv_state, kv, extra, (w0, w1)

    def mix_p(x):
        c, u_last = _glu_conv(x, _row(norm_mix[0]), w_pw1b, _row(b_pw1[0]), w_dw[0], _row(b_dw[0]),
                              seq_len=t, tm=512, tn=256)
        return _ln_pw2(c, x, *ln_pw2_w, tm=512), u_last[:, HALO - (CONV_WIDTH - 1):]

    def attend_p(q, kv, x):
        x3 = _attn_prompt(q.reshape(b, t, d), kv.reshape(b, t, -1), x.reshape(b, t, d), w_ob,
                          sinks[0], tq=512)
        return x3.reshape(b * t, d), None

    pos_s = PAST_LEN + jnp.arange(s, dtype=jnp.int32)

    def mix_s(x):
        u = _glu(x, _row(norm_mix[0]), w_pw1b, _row(b_pw1[0]), tm=nb * s, tn=512)
        full = jnp.concatenate([state_conv[0], u.reshape(nb, s, d)], axis=1)
        x = _conv_sample(full, x, w_dw[0], _row(b_dw[0]), *ln_pw2_w)
        return x, full[:, -(CONV_WIDTH - 1):]

    def attend_s(q, kv, x):
        x, wk, wv = _attn_sample(q.reshape(nb, s, d), kv.reshape(nb, s, -1), cache_k, cache_v,
                                 x, w_ob, sinks[0], gb=8)
        return x, (wk, wv)

    tables_s = tuple(jnp.tile(tb, (nb, 1)) for tb in _rope_tables(pos_s))
    y_s, conv_s_state, _, (win_k_s, win_v_s), mlp_b = tokens(
        x_sample.reshape(nb * s, d), mix=mix_s, attend=attend_s, tables=tables_s,
        table_blocks=1, tm_qkv=nb * s, mlp_weights=lambda l: dict(wu=w_up, wd=w_down, layer=l))

    tm_p = _tile(b * t, 512)
    y_p, conv_p_state, kv_p, _, _ = tokens(
        x_prompt.reshape(b * t, d), mix=mix_p, attend=attend_p,
        tables=_rope_tables(jnp.arange(t, dtype=jnp.int32)), table_blocks=t // tm_p, tm_qkv=tm_p,
        mlp_weights=lambda l: dict(wu=mlp_b[l][0], wd=mlp_b[l][1]))
    kv_p = kv_p.reshape(b, t, -1)[:, t - WINDOW:].reshape(b, WINDOW, 2, N_KV_HEADS, HEAD_DIM)

    return (y_p.reshape(b, t, d), y_s.reshape(nb, s, d), conv_p_state[None], conv_s_state[None],
            kv_p[:, :, 0], kv_p[:, :, 1], win_k_s, win_v_s)
```

```python
import functools

import jax
import jax.numpy as jnp
from jax import lax
from jax.experimental import pallas as pl
from jax.experimental.pallas import tpu as pltpu

F32 = jnp.float32
BF16 = jnp.bfloat16

EPS = 1e-6
HEAD_DIM = 64
N_KV_HEADS = 8
ROT_DIM = HEAD_DIM // 4
ROPE_THETA = 500000.0
WINDOW = 128
CONV_WIDTH = 31
PAST_LEN = 16384

LANES = 128
SUBLANES = 8
VMEM_LIMIT_BYTES = 54 * 1024 * 1024
MLP_VMEM_LIMIT_BYTES = 60 * 1024 * 1024

HALO = -(-(CONV_WIDTH - 1) // SUBLANES) * SUBLANES
TAP0 = HALO - (CONV_WIDTH - 1)


def _params(*semantics, vmem_limit_bytes=VMEM_LIMIT_BYTES):
    return pltpu.CompilerParams(dimension_semantics=semantics, vmem_limit_bytes=vmem_limit_bytes)


def _resident(shape):
    return pl.BlockSpec(shape, lambda *_: (0,) * len(shape), pipeline_mode=pl.Buffered(1))


def _rms_scale(x):
    return lax.rsqrt(jnp.mean(x * x, axis=-1, keepdims=True) + EPS)


def _glu_kernel(x_ref, g_ref, wa_ref, wg_ref, ba_ref, bg_ref, u_ref, h_ref):
    @pl.when(pl.program_id(1) == 0)
    def _():
        x = x_ref[...]
        h_ref[...] = (x * _rms_scale(x) * g_ref[...]).astype(BF16)

    h = h_ref[...]
    a = jnp.dot(h, wa_ref[...].astype(BF16), preferred_element_type=F32) + ba_ref[...]
    gate = jnp.dot(h, wg_ref[...].astype(BF16), preferred_element_type=F32) + bg_ref[...]
    u_ref[...] = a * jax.nn.sigmoid(gate)


def _glu(x, g, w, b, *, tm, tn):
    m, d = x.shape
    nj = d // tn
    return pl.pallas_call(
        _glu_kernel,
        grid=(m // tm, nj),
        in_specs=[
            pl.BlockSpec((tm, d), lambda i, j: (i, 0)),
            pl.BlockSpec((1, d), lambda i, j: (0, 0)),
            pl.BlockSpec((d, tn), lambda i, j: (0, j)),
            pl.BlockSpec((d, tn), lambda i, j: (0, j + nj)),
            pl.BlockSpec((1, tn), lambda i, j: (0, j)),
            pl.BlockSpec((1, tn), lambda i, j: (0, j + nj)),
        ],
        out_specs=pl.BlockSpec((tm, tn), lambda i, j: (i, j)),
        out_shape=jax.ShapeDtypeStruct((m, d), F32),
        scratch_shapes=[pltpu.VMEM((tm, d), BF16)],
        compiler_params=_params("parallel", "arbitrary"),
        name="glu",
    )(x, g, w, w, b, b)


def _ln_silu_pw2(c, x, lng_ref, lnb_ref, w2_ref, b2_ref):
    mu = jnp.mean(c, axis=-1, keepdims=True)
    cc = c - mu
    var = jnp.mean(cc * cc, axis=-1, keepdims=True)
    y = cc * lax.rsqrt(var + EPS) * lng_ref[...] + lnb_ref[...]
    y = y * jax.nn.sigmoid(y)
    return x + jnp.dot(y.astype(BF16), w2_ref[...], preferred_element_type=F32) + b2_ref[...]


CONV_ROWS = 64
CONV_LANES = 128


def _conv_taps(win, wdw_ref, lanes):
    acc = jnp.zeros((CONV_ROWS, CONV_LANES), F32)
    for res in range(SUBLANES):
        shifted = win if res == 0 else pltpu.roll(win, CONV_ROWS + HALO - res, 0)
        for k in range(CONV_WIDTH):
            if (TAP0 + k) % SUBLANES == res:
                a = (TAP0 + k) // SUBLANES * SUBLANES
                acc = acc + shifted[a:a + CONV_ROWS] * wdw_ref[k:k + 1, lanes]
    return acc


def _glu_conv_kernel(x_ref, g_ref, w_ref, b_ref, wdw_ref, bdw_ref, c_ref, ulast_ref,
                     h_ref, win_a, win_b, hal_ref, *, tiles_per_seq):
    tm, d = x_ref.shape
    tn = win_a.shape[1]
    nj = d // tn
    x = x_ref[...]
    h_ref[...] = (x * _rms_scale(x) * g_ref[...]).astype(BF16)
    first = pl.program_id(0) % tiles_per_seq == 0

    @pl.when(pl.program_id(0) == 0)
    def _():
        hal_ref[...] = jnp.zeros_like(hal_ref)

    def glu(j, win_ref):
        cols = pl.ds(pl.multiple_of(j * tn, tn), tn)
        gcols = pl.ds(pl.multiple_of(d + j * tn, tn), tn)
        a = jnp.dot(h_ref[...], w_ref[:, cols], preferred_element_type=F32) + b_ref[:, cols]
        gate = jnp.dot(h_ref[...], w_ref[:, gcols], preferred_element_type=F32) + b_ref[:, gcols]
        u = a * jax.nn.sigmoid(gate)
        tail = u[tm - HALO:, :]
        win_ref[0:HALO, :] = jnp.where(first, 0.0, hal_ref[:, cols])
        win_ref[HALO:, :] = u
        hal_ref[:, cols] = tail
        ulast_ref[0, :, cols] = tail

    def conv(j, win_ref):
        for rc in range(tm // CONV_ROWS):
            for lc in range(tn // CONV_LANES):
                lanes = slice(lc * CONV_LANES, (lc + 1) * CONV_LANES)
                out_lanes = pl.ds(pl.multiple_of(j * tn + lc * CONV_LANES, CONV_LANES), CONV_LANES)
                win = win_ref[rc * CONV_ROWS:(rc + 1) * CONV_ROWS + HALO, lanes]
                c_ref[rc * CONV_ROWS:(rc + 1) * CONV_ROWS, out_lanes] = (
                    _conv_taps(win, wdw_ref, out_lanes) + bdw_ref[:, out_lanes])

    def tile_pair(p, carry):
        glu(2 * p + 1, win_b)
        conv(2 * p, win_a)
        glu(2 * p + 2, win_a)
        conv(2 * p + 1, win_b)
        return carry

    glu(0, win_a)
    lax.fori_loop(0, nj // 2 - 1, tile_pair, 0)
    glu(nj - 1, win_b)
    conv(nj - 2, win_a)
    conv(nj - 1, win_b)


def _glu_conv(x, g, w, b, wdw, bdw, *, seq_len, tm, tn):
    m, d = x.shape
    tiles_per_seq = seq_len // tm
    assert seq_len % tm == 0 and (d // tn) % 2 == 0
    return pl.pallas_call(
        functools.partial(_glu_conv_kernel, tiles_per_seq=tiles_per_seq),
        grid=(m // tm,),
        in_specs=[
            pl.BlockSpec((tm, d), lambda i: (i, 0)),
            _resident((1, d)),
            _resident((d, 2 * d)),
            _resident((1, 2 * d)),
            _resident((CONV_WIDTH, d)),
            _resident((1, d)),
        ],
        out_specs=[
            pl.BlockSpec((tm, d), lambda i: (i, 0)),
            pl.BlockSpec((1, HALO, d), lambda i: (i // tiles_per_seq, 0, 0)),
        ],
        out_shape=[jax.ShapeDtypeStruct((m, d), F32),
                   jax.ShapeDtypeStruct((m // seq_len, HALO, d), F32)],
        scratch_shapes=[pltpu.VMEM((tm, d), BF16), pltpu.VMEM((HALO + tm, tn), F32),
                        pltpu.VMEM((HALO + tm, tn), F32), pltpu.VMEM((HALO, d), F32)],
        compiler_params=_params("arbitrary"),
        name="glu_conv",
    )(x, g, w, b, wdw, bdw)


def _ln_pw2_kernel(c_ref, x_ref, lng_ref, lnb_ref, w2_ref, b2_ref, o_ref):
    o_ref[...] = _ln_silu_pw2(c_ref[...], x_ref[...], lng_ref, lnb_ref, w2_ref, b2_ref)


def _ln_pw2(c, x, lng, lnb, w2, b2, *, tm):
    m, d = x.shape
    rows = pl.BlockSpec((tm, d), lambda i: (i, 0))
    return pl.pallas_call(
        _ln_pw2_kernel,
        grid=(m // tm,),
        in_specs=[rows, rows, _resident((1, d)), _resident((1, d)), _resident((d, d)),
                  _resident((1, d))],
        out_specs=rows,
        out_shape=jax.ShapeDtypeStruct((m, d), F32),
        compiler_params=_params("parallel"),
        name="ln_pw2",
    )(c, x, lng, lnb, w2, b2)


def _conv_sample_kernel(full_ref, x_ref, wdw_ref, bdw_ref, lng_ref, lnb_ref, w2_ref, b2_ref,
                        o_ref, c_ref):
    nb, rows, d = full_ref.shape
    s = rows - (CONV_WIDTH - 1)
    for lc in range(d // CONV_LANES):
        lanes = slice(lc * CONV_LANES, (lc + 1) * CONV_LANES)
        acc = jnp.zeros((nb, s, CONV_LANES), F32)
        for k in range(CONV_WIDTH):
            acc = acc + full_ref[:, k:k + s, lanes] * wdw_ref[k:k + 1, lanes]
        c_ref[:, lanes] = acc.reshape(nb * s, CONV_LANES) + bdw_ref[:, lanes]
    o_ref[...] = _ln_silu_pw2(c_ref[...], x_ref[...], lng_ref, lnb_ref, w2_ref, b2_ref)


def _conv_sample(full, x, wdw, bdw, lng, lnb, w2, b2):
    m, d = x.shape
    whole = lambda a: pl.BlockSpec(a.shape, lambda i: (0,) * a.ndim)
    args = (full, x, wdw, bdw, lng, lnb, w2, b2)
    return pl.pallas_call(
        _conv_sample_kernel,
        grid=(1,),
        in_specs=[whole(a) for a in args],
        out_specs=pl.BlockSpec((m, d), lambda i: (0, 0)),
        out_shape=jax.ShapeDtypeStruct((m, d), F32),
        scratch_shapes=[pltpu.VMEM((m, d), F32)],
        compiler_params=_params("arbitrary"),
        name="conv_sample",
    )(*args)


def _mlp_kernel(x_ref, g_ref, wu_ref, wd_ref, fg_ref, o_ref, *rest, final, emit_weights):
    if emit_weights:
        wu_out_ref, wd_out_ref, h_ref = rest
    else:
        (h_ref,) = rest
    f = pl.program_id(1)

    @pl.when(f == 0)
    def _():
        x = x_ref[...]
        h_ref[...] = (x * _rms_scale(x) * g_ref[...]).astype(BF16)
        o_ref[...] = x

    wu = wu_ref[...].astype(BF16)
    wd = wd_ref[...].astype(BF16)
    if emit_weights:
        wu_out_ref[...] = wu
        wd_out_ref[...] = wd
    hid = jnp.dot(h_ref[...], wu, preferred_element_type=F32)
    hid = jnp.square(jnp.maximum(hid, 0.0)).astype(BF16)
    o_ref[...] += jnp.dot(hid, wd, preferred_element_type=F32)

    if final:
        @pl.when(f == pl.num_programs(1) - 1)
        def _():
            y = o_ref[...]
            o_ref[...] = y * _rms_scale(y) * fg_ref[...]


def _mlp(x, g, *, wu, wd, fg, final, tm, tf, layer=None):
    m, d = x.shape
    emit_weights = layer is not None
    if emit_weights:
        assert m == tm, "each weight tile must be visited exactly once"
        dff = wu.shape[2]
        w_specs = [pl.BlockSpec((None, d, tf), lambda i, f: (layer, 0, f)),
                   pl.BlockSpec((None, tf, d), lambda i, f: (layer, f, 0))]
    else:
        dff = wu.shape[1]
        w_specs = [pl.BlockSpec((d, tf), lambda i, f: (0, f)),
                   pl.BlockSpec((tf, d), lambda i, f: (f, 0))]
    out_specs = [pl.BlockSpec((tm, d), lambda i, f: (i, 0))]
    out_shape = [jax.ShapeDtypeStruct((m, d), F32)]
    if emit_weights:
        out_specs += [pl.BlockSpec((d, tf), lambda i, f: (0, f)),
                      pl.BlockSpec((tf, d), lambda i, f: (f, 0))]
        out_shape += [jax.ShapeDtypeStruct((d, dff), BF16), jax.ShapeDtypeStruct((dff, d), BF16)]
    out = pl.pallas_call(
        functools.partial(_mlp_kernel, final=final, emit_weights=emit_weights),
        grid=(m // tm, dff // tf),
        in_specs=[
            pl.BlockSpec((tm, d), lambda i, f: (i, 0)),
            pl.BlockSpec((1, d), lambda i, f: (0, 0)),
            *w_specs,
            pl.BlockSpec((1, d), lambda i, f: (0, 0)),
        ],
        out_specs=out_specs,
        out_shape=out_shape,
        scratch_shapes=[pltpu.VMEM((tm, d), BF16)],
        compiler_params=_params("parallel", "arbitrary", vmem_limit_bytes=MLP_VMEM_LIMIT_BYTES),
        name="mlp_final" if final else "mlp",
    )(x, g, wu, wd, fg)
    return out if emit_weights else out[0]


def _rope_tables(pos):
    half = ROT_DIM // 2
    inv = ROPE_THETA ** (-jnp.arange(half, dtype=F32) / half)
    ang = pos.astype(F32)[:, None] * inv[None, :]
    cos, sin = lax.optimization_barrier((jnp.cos(ang), jnp.sin(ang)))
    ones = jnp.ones((pos.shape[0], HEAD_DIM - ROT_DIM), F32)
    zeros = jnp.zeros_like(ones)
    zh = jnp.zeros_like(sin)
    per_head = lambda parts: jnp.tile(jnp.concatenate(parts, axis=1), (1, LANES // HEAD_DIM))
    return per_head([cos, cos, ones]), per_head([zh, sin, zeros]), per_head([-sin, zh, zeros])


def _rope_block(blk, cos, sin_up, sin_dn):
    half = ROT_DIM // 2
    return (blk * cos + pltpu.roll(blk, half, 1) * sin_up
            + pltpu.roll(blk, LANES - half, 1) * sin_dn)


PROJ_COLS = 1024


def _qkv_kernel(x_ref, gq_ref, gk_ref, w_ref, cos_ref, sup_ref, sdn_ref, q_ref, kv_ref):
    d = x_ref.shape[1]
    k_width = kv_ref.shape[1] // 2
    x = x_ref[...]
    xn = x * _rms_scale(x)
    hq = (xn * gq_ref[...]).astype(BF16)
    hk = (xn * gk_ref[...]).astype(BF16)
    tables = (cos_ref[...], sup_ref[...], sdn_ref[...])

    def project(h, w_col, out_ref, out_col, width, rope_cols):
        res = jnp.dot(h, w_ref[:, w_col:w_col + width], preferred_element_type=F32)
        for c in range(0, rope_cols, LANES):
            out_ref[:, out_col + c:out_col + c + LANES] = _rope_block(res[:, c:c + LANES], *tables)
        if rope_cols < width:
            out_ref[:, out_col + rope_cols:out_col + width] = res[:, rope_cols:]

    for c0 in range(0, d, PROJ_COLS):
        project(hq, c0, q_ref, c0, PROJ_COLS, PROJ_COLS)
    project(hk, d, kv_ref, 0, 2 * k_width, k_width)


def _qkv(x, gq, gk, w, tables, *, tm, table_blocks):
    m, d = x.shape
    n_kv = w.shape[1] - d
    tspec = pl.BlockSpec((tm, LANES), lambda i: (i % table_blocks, 0))
    return pl.pallas_call(
        _qkv_kernel,
        grid=(m // tm,),
        in_specs=[
            pl.BlockSpec((tm, d), lambda i: (i, 0)),
            _resident((1, d)),
            _resident((1, d)),
            _resident(w.shape),
            tspec, tspec, tspec,
        ],
        out_specs=[
            pl.BlockSpec((tm, d), lambda i: (i, 0)),
            pl.BlockSpec((tm, n_kv), lambda i: (i, 0)),
        ],
        out_shape=[jax.ShapeDtypeStruct((m, d), F32), jax.ShapeDtypeStruct((m, n_kv), F32)],
        compiler_params=_params("parallel"),
        name="qkv",
    )(x, gq, gk, w, *tables)


def _qk(q, k):
    return lax.dot_general(q, k, (((1,), (1,)), ((), ())), preferred_element_type=F32)


def _attn_prompt_kernel(q_ref, kvp_ref, kvc_ref, x_ref, wo_ref, sink_ref, o_ref, kv_ref, ot_ref):
    blk = kvp_ref.shape[1]
    n_blk = q_ref.shape[1] // blk
    k_width = kvc_ref.shape[2] // 2
    group = q_ref.shape[2] // k_width
    n_kv_heads = k_width // HEAD_DIM
    kv_ref[0:blk, :] = kvp_ref[0].astype(BF16)
    kv_ref[blk:, :] = kvc_ref[0].astype(BF16)

    key = lax.broadcasted_iota(jnp.int32, (2 * blk, group * blk), 0)
    qry = lax.broadcasted_iota(jnp.int32, (2 * blk, group * blk), 1) % blk
    own = (key >= blk) & (key - blk <= qry)
    bias_inner = jnp.where(((key < blk) & (key > qry)) | own, 0.0, -jnp.inf).astype(F32)
    bias_first = jnp.where(pl.program_id(1) > 0, bias_inner, jnp.where(own, 0.0, -jnp.inf))
    head_of_lane = lax.broadcasted_iota(jnp.int32, (1, group * blk), 1) // blk
    scale = HEAD_DIM ** -0.5

    def q_block(qb, carry):
        row0 = pl.multiple_of(qb * blk, blk)
        bias = jnp.where(qb == 0, bias_first, bias_inner)

        def scores(j):
            kk = kv_ref[pl.ds(row0, 2 * blk), j * HEAD_DIM:(j + 1) * HEAD_DIM]
            qs = jnp.concatenate(
                [q_ref[0, pl.ds(row0, blk), h * HEAD_DIM:(h + 1) * HEAD_DIM]
                 for h in range(j * group, (j + 1) * group)], axis=0)
            return _qk(kk, (qs * scale).astype(BF16)) + bias

        st = scores(0)
        for j in range(n_kv_heads):
            st_next = scores(j + 1) if j + 1 < n_kv_heads else None
            vv = kv_ref[pl.ds(row0, 2 * blk), k_width + j * HEAD_DIM:k_width + (j + 1) * HEAD_DIM]
            sink = jnp.zeros((1, group * blk), F32)
            for g in range(group):
                sink = jnp.where(head_of_lane == g, sink_ref[j * group + g], sink)
            m = jnp.maximum(jnp.max(st, axis=0, keepdims=True), sink)
            p = jnp.exp(st - m)
            denom = jnp.sum(p, axis=0, keepdims=True) + jnp.exp(sink - m)
            ot = lax.dot_general(vv, p.astype(BF16), (((0,), (0,)), ((), ())),
                                 preferred_element_type=F32) * (1.0 / denom)
            for g in range(group):
                h = j * group + g
                ot_ref[h * HEAD_DIM:(h + 1) * HEAD_DIM, pl.ds(row0, blk)] = (
                    ot[:, g * blk:(g + 1) * blk].astype(BF16))
            st = st_next
        return carry

    lax.fori_loop(0, n_blk, q_block, 0)

    proj = lax.dot_general(ot_ref[...], wo_ref[...], (((0,), (0,)), ((), ())),
                           preferred_element_type=F32)
    o_ref[0] = x_ref[0] + proj


def _attn_prompt(q, kv, x, wo, sinks, *, tq):
    b, t, d = x.shape
    n_kv = kv.shape[2]
    blk = WINDOW
    per = tq // blk
    return pl.pallas_call(
        _attn_prompt_kernel,
        grid=(b, t // tq),
        in_specs=[
            pl.BlockSpec((1, tq, d), lambda bi, n: (bi, n, 0)),
            pl.BlockSpec((1, blk, n_kv), lambda bi, n: (bi, jnp.maximum(n * per - 1, 0), 0)),
            pl.BlockSpec((1, tq, n_kv), lambda bi, n: (bi, n, 0)),
            pl.BlockSpec((1, tq, d), lambda bi, n: (bi, n, 0)),
            _resident((d, d)),
            pl.BlockSpec(memory_space=pltpu.SMEM),
        ],
        out_specs=pl.BlockSpec((1, tq, d), lambda bi, n: (bi, n, 0)),
        out_shape=jax.ShapeDtypeStruct((b, t, d), F32),
        scratch_shapes=[pltpu.VMEM((blk + tq, n_kv), BF16), pltpu.VMEM((d, tq), BF16)],
        compiler_params=_params("parallel", "arbitrary"),
        name="attn_prompt",
    )(q, kv, kv, x, wo, sinks)


def _attn_sample_kernel(q_ref, kvn_ref, ck_ref, cv_ref, x_ref, wo_ref, sink_ref,
                        o_ref, wk_ref, wv_ref, ocat_ref):
    gb, s, d = q_ref.shape
    _, n_kv_heads, hd, win = ck_ref.shape
    k_width = n_kv_heads * hd
    group = d // k_width
    step = pl.program_id(0)

    lane = lax.broadcasted_iota(jnp.int32, (1, win), 1)
    is_new = lane >= win - s
    qi = lax.broadcasted_iota(jnp.int32, (group * s, win), 0) % s
    kl = lax.broadcasted_iota(jnp.int32, (group * s, win), 1)
    bias_new = jnp.where(kl + s <= qi + win, 0.0, -jnp.inf).astype(F32)
    bias_old = jnp.where((kl < s) & (kl > qi), 0.0, -jnp.inf).astype(F32)
    head_of_row = lax.broadcasted_iota(jnp.int32, (group * s, 1), 0) // s
    scale = hd ** -0.5
    nt = (((1,), (1,)), ((), ()))

    def one_batch(bb, carry):
        row0 = pl.multiple_of((step * gb + bb) * s, s)
        qb = q_ref[bb]
        kn = kvn_ref[bb]
        pad = jnp.zeros((win - s, k_width), F32)
        kn_t = jnp.concatenate([pad, kn[:, :k_width]], axis=0).T
        vn_t = jnp.concatenate([pad, kn[:, k_width:]], axis=0).T

        def window(old, new_t, j):
            shifted = pltpu.roll(old, win - s, 1)
            return jnp.where(is_new, new_t[j * hd:(j + 1) * hd, :], shifted)

        def scores(j):
            k_old = ck_ref[bb, j]
            k_win = window(k_old, kn_t, j)
            wk_ref[bb, j] = k_win
            qs = jnp.concatenate([qb[:, h * hd:(h + 1) * hd]
                                  for h in range(j * group, (j + 1) * group)], axis=0)
            qs = (qs * scale).astype(BF16)
            s_new = jnp.dot(qs, k_win.astype(BF16), preferred_element_type=F32) + bias_new
            s_old = jnp.dot(qs, k_old.astype(BF16), preferred_element_type=F32) + bias_old
            return s_new, s_old

        sc = scores(0)
        for j in range(n_kv_heads):
            sc_next = scores(j + 1) if j + 1 < n_kv_heads else None
            v_old = cv_ref[bb, j]
            v_win = window(v_old, vn_t, j)
            wv_ref[bb, j] = v_win
            sink = jnp.zeros((group * s, 1), F32)
            for g in range(group):
                sink = jnp.where(head_of_row == g, sink_ref[j * group + g], sink)
            s_new, s_old = sc
            m = jnp.maximum(jnp.maximum(jnp.max(s_new, axis=-1, keepdims=True),
                                        jnp.max(s_old, axis=-1, keepdims=True)), sink)
            p_new = jnp.exp(s_new - m)
            p_old = jnp.exp(s_old - m)
            denom = (jnp.sum(p_new, axis=-1, keepdims=True) + jnp.sum(p_old, axis=-1, keepdims=True)
                     + jnp.exp(sink - m))
            o = (lax.dot_general(p_new.astype(BF16), v_win.astype(BF16), nt,
                                 preferred_element_type=F32)
                 + lax.dot_general(p_old.astype(BF16), v_old.astype(BF16), nt,
                                   preferred_element_type=F32)) * (1.0 / denom)
            for g in range(group):
                h = j * group + g
                ocat_ref[pl.ds(row0, s), h * hd:(h + 1) * hd] = o[g * s:(g + 1) * s]
            sc = sc_next
        return carry

    lax.fori_loop(0, gb, one_batch, 0)

    @pl.when(step == pl.num_programs(0) - 1)
    def _():
        o_ref[...] = x_ref[...] + jnp.dot(ocat_ref[...].astype(BF16), wo_ref[...],
                                          preferred_element_type=F32)


def _attn_sample(q, kvn, ck, cv, x, wo, sinks, *, gb):
    nb, s, d = q.shape
    m = nb * s
    to_lanes, from_lanes = (0, 2, 3, 1), (0, 3, 1, 2)
    ck_t, cv_t = ck.transpose(to_lanes), cv.transpose(to_lanes)
    cache_spec = pl.BlockSpec((gb,) + ck_t.shape[1:], lambda i: (i, 0, 0, 0))
    x, wk_t, wv_t = pl.pallas_call(
        _attn_sample_kernel,
        grid=(nb // gb,),
        in_specs=[
            pl.BlockSpec((gb, s, d), lambda i: (i, 0, 0)),
            pl.BlockSpec((gb, s, kvn.shape[2]), lambda i: (i, 0, 0)),
            cache_spec,
            cache_spec,
            _resident((m, d)),
            _resident((d, d)),
            pl.BlockSpec(memory_space=pltpu.SMEM),
        ],
        out_specs=[pl.BlockSpec((m, d), lambda i: (0, 0)), cache_spec, cache_spec],
        out_shape=[
            jax.ShapeDtypeStruct((m, d), F32),
            jax.ShapeDtypeStruct(ck_t.shape, F32),
            jax.ShapeDtypeStruct(cv_t.shape, F32),
        ],
        scratch_shapes=[pltpu.VMEM((m, d), F32)],
        compiler_params=_params("arbitrary"),
        name="attn_sample",
    )(q, kvn, ck_t, cv_t, x, wo, sinks)
    return x, wk_t.transpose(from_lanes), wv_t.transpose(from_lanes)


def _row(v):
    return v.reshape(1, -1)


def _as_list(out):
    return list(out) if isinstance(out, (list, tuple)) else [out]


def _tile(m, want):
    return want if m % want == 0 else m


def kernel(x_prompt, x_sample, state_conv, cache_k, cache_v, norm_mix, w_pw1, b_pw1, w_dw, b_dw,
           conv_ln_g, conv_ln_b, w_pw2, b_pw2, kv_norm, w_k, w_v, w_q, w_o, sinks, norm_mlp,
           w_up, w_down, final_norm):
    b, t, d = x_prompt.shape
    nb, s, _ = x_sample.shape
    win = cache_k.shape[1]
    depth = norm_mlp.shape[0]
    n_conv = w_pw1.shape[0]
    assert depth == 2 and n_conv == 1 and w_q.shape[0] == 1, "one conv layer then one attention layer"
    assert t % WINDOW == 0 and win == WINDOW

    w_pw1b = w_pw1[0].astype(BF16)
    w_pw2b = w_pw2[0].astype(BF16)
    w_ob = w_o[0].astype(BF16)
    w_qkvb = jnp.concatenate([w_q[0], w_k, w_v], axis=1).astype(BF16)
    ln_pw2_w = (_row(conv_ln_g[0]), _row(conv_ln_b[0]), w_pw2b, _row(b_pw2[0]))

    def tokens(x, *, mix, attend, tables, table_blocks, tm_qkv, mlp_weights):
        m = x.shape[0]
        tm = _tile(m, 1024)
        tf = 512 if "layer" in mlp_weights(0) else 1024
        fg = _row(final_norm)
        x, conv_state = mix(x)
        x, *w0 = _as_list(_mlp(x, _row(norm_mlp[0]), fg=fg, final=False, tm=tm, tf=tf,
                               **mlp_weights(0)))
        q, kv = _qkv(x, _row(norm_mix[1]), _row(kv_norm), w_qkvb, tables, tm=tm_qkv,
                     table_blocks=table_blocks)
        x, extra = attend(q, kv, x)
        y, *w1 = _as_list(_mlp(x, _row(norm_mlp[1]), fg=fg, final=True, tm=tm, tf=tf,
                               **mlp_weights(1)))
        return y, conv_state, kv, extra, (w0, w1)

    def mix_p(x):
        c, u_last = _glu_conv(x, _row(norm_mix[0]), w_pw1b, _row(b_pw1[0]), w_dw[0], _row(b_dw[0]),
                              seq_len=t, tm=512, tn=256)
        return _ln_pw2(c, x, *ln_pw2_w, tm=512), u_last[:, HALO - (CONV_WIDTH - 1):]

    def attend_p(q, kv, x):
        x3 = _attn_prompt(q.reshape(b, t, d), kv.reshape(b, t, -1), x.reshape(b, t, d), w_ob,
                          sinks[0], tq=512)
        return x3.reshape(b * t, d), None

    pos_s = PAST_LEN + jnp.arange(s, dtype=jnp.int32)

    def mix_s(x):
        u = _glu(x, _row(norm_mix[0]), w_pw1b, _row(b_pw1[0]), tm=nb * s, tn=512)
        full = jnp.concatenate([state_conv[0], u.reshape(nb, s, d)], axis=1)
        x = _conv_sample(full, x, w_dw[0], _row(b_dw[0]), *ln_pw2_w)
        return x, full[:, -(CONV_WIDTH - 1):]

    def attend_s(q, kv, x):
        x, wk, wv = _attn_sample(q.reshape(nb, s, d), kv.reshape(nb, s, -1), cache_k, cache_v,
                                 x, w_ob, sinks[0], gb=8)
        return x, (wk, wv)

    tables_s = tuple(jnp.tile(tb, (nb, 1)) for tb in _rope_tables(pos_s))
    y_s, conv_s_state, _, (win_k_s, win_v_s), mlp_b = tokens(
        x_sample.reshape(nb * s, d), mix=mix_s, attend=attend_s, tables=tables_s,
        table_blocks=1, tm_qkv=nb * s, mlp_weights=lambda l: dict(wu=w_up, wd=w_down, layer=l))

    tm_p = _tile(b * t, 512)
    y_p, conv_p_state, kv_p, _, _ = tokens(
        x_prompt.reshape(b * t, d), mix=mix_p, attend=attend_p,
        tables=_rope_tables(jnp.arange(t, dtype=jnp.int32)), table_blocks=t // tm_p, tm_qkv=tm_p,
        mlp_weights=lambda l: dict(wu=mlp_b[l][0], wd=mlp_b[l][1]))
    kv_p = kv_p.reshape(b, t, -1)[:, t - WINDOW:].reshape(b, WINDOW, 2, N_KV_HEADS, HEAD_DIM)

    return (y_p.reshape(b, t, d), y_s.reshape(nb, s, d), conv_p_state[None], conv_s_state[None],
            kv_p[:, :, 0], kv_p[:, :, 1], win_k_s, win_v_s)
```

```python
import functools

import jax
import jax.numpy as jnp
from jax import lax
from jax.experimental import pallas as pl
from jax.experimental.pallas import tpu as pltpu

F32 = jnp.float32
BF16 = jnp.bfloat16

EPS = 1e-6
HEAD_DIM = 64
N_KV_HEADS = 8
ROT_DIM = HEAD_DIM // 4
ROPE_THETA = 500000.0
WINDOW = 128
CONV_WIDTH = 31
PAST_LEN = 16384

LANES = 128
SUBLANES = 8
VMEM_LIMIT_BYTES = 54 * 1024 * 1024
MLP_VMEM_LIMIT_BYTES = 60 * 1024 * 1024

HALO = -(-(CONV_WIDTH - 1) // SUBLANES) * SUBLANES
TAP0 = HALO - (CONV_WIDTH - 1)


def _params(*semantics, vmem_limit_bytes=VMEM_LIMIT_BYTES):
    return pltpu.CompilerParams(dimension_semantics=semantics, vmem_limit_bytes=vmem_limit_bytes)


def _resident(shape):
    return pl.BlockSpec(shape, lambda *_: (0,) * len(shape), pipeline_mode=pl.Buffered(1))


def _rms_scale(x):
    return lax.rsqrt(jnp.mean(x * x, axis=-1, keepdims=True) + EPS)


def _glu_kernel(x_ref, g_ref, wa_ref, wg_ref, ba_ref, bg_ref, u_ref, wa_out_ref, wg_out_ref, h_ref):
    @pl.when(pl.program_id(0) == 0)
    def _():
        x = x_ref[...]
        h_ref[...] = (x * _rms_scale(x) * g_ref[...]).astype(BF16)

    h = h_ref[...]
    wa = wa_ref[...].astype(BF16)
    wg = wg_ref[...].astype(BF16)
    wa_out_ref[...] = wa
    wg_out_ref[...] = wg
    a = jnp.dot(h, wa, preferred_element_type=F32) + ba_ref[...]
    gate = jnp.dot(h, wg, preferred_element_type=F32) + bg_ref[...]
    u_ref[...] = a * jax.nn.sigmoid(gate)


def _glu(x, g, w, b, *, tn):
    m, d = x.shape
    nj = d // tn
    return pl.pallas_call(
        _glu_kernel,
        grid=(nj,),
        in_specs=[
            pl.BlockSpec((m, d), lambda j: (0, 0)),
            pl.BlockSpec((1, d), lambda j: (0, 0)),
            pl.BlockSpec((d, tn), lambda j: (0, j)),
            pl.BlockSpec((d, tn), lambda j: (0, j + nj)),
            pl.BlockSpec((1, tn), lambda j: (0, j)),
            pl.BlockSpec((1, tn), lambda j: (0, j + nj)),
        ],
        out_specs=[pl.BlockSpec((m, tn), lambda j: (0, j)),
                   pl.BlockSpec((d, tn), lambda j: (0, j)),
                   pl.BlockSpec((d, tn), lambda j: (0, j))],
        out_shape=[jax.ShapeDtypeStruct((m, d), F32), jax.ShapeDtypeStruct((d, d), BF16),
                   jax.ShapeDtypeStruct((d, d), BF16)],
        scratch_shapes=[pltpu.VMEM((m, d), BF16)],
        compiler_params=_params("arbitrary"),
        name="glu",
    )(x, g, w, w, b, b)


def _ln_silu_pw2(c, x, lng_ref, lnb_ref, w2_ref, b2_ref):
    mu = jnp.mean(c, axis=-1, keepdims=True)
    cc = c - mu
    var = jnp.mean(cc * cc, axis=-1, keepdims=True)
    y = cc * lax.rsqrt(var + EPS) * lng_ref[...] + lnb_ref[...]
    y = y * jax.nn.sigmoid(y)
    return x + jnp.dot(y.astype(BF16), w2_ref[...], preferred_element_type=F32) + b2_ref[...]


CONV_ROWS = 64
CONV_LANES = 128


def _conv_taps(win, wdw_ref, lanes):
    acc = jnp.zeros((CONV_ROWS, CONV_LANES), F32)
    for res in range(SUBLANES):
        shifted = win if res == 0 else pltpu.roll(win, CONV_ROWS + HALO - res, 0)
        for k in range(CONV_WIDTH):
            if (TAP0 + k) % SUBLANES == res:
                a = (TAP0 + k) // SUBLANES * SUBLANES
                acc = acc + shifted[a:a + CONV_ROWS] * wdw_ref[k:k + 1, lanes]
    return acc


def _glu_conv_kernel(x_ref, g_ref, wa_ref, wg_ref, b_ref, wdw_ref, bdw_ref, c_ref, ulast_ref,
                     h_ref, win_a, win_b, hal_ref, *, tiles_per_seq):
    tm, d = x_ref.shape
    tn = win_a.shape[1]
    nj = d // tn
    x = x_ref[...]
    h_ref[...] = (x * _rms_scale(x) * g_ref[...]).astype(BF16)
    first = pl.program_id(0) % tiles_per_seq == 0

    @pl.when(pl.program_id(0) == 0)
    def _():
        hal_ref[...] = jnp.zeros_like(hal_ref)

    def glu(j, win_ref):
        cols = pl.ds(pl.multiple_of(j * tn, tn), tn)
        gcols = pl.ds(pl.multiple_of(d + j * tn, tn), tn)
        a = jnp.dot(h_ref[...], wa_ref[:, cols], preferred_element_type=F32) + b_ref[:, cols]
        gate = jnp.dot(h_ref[...], wg_ref[:, cols], preferred_element_type=F32) + b_ref[:, gcols]
        u = a * jax.nn.sigmoid(gate)
        tail = u[tm - HALO:, :]
        win_ref[0:HALO, :] = jnp.where(first, 0.0, hal_ref[:, cols])
        win_ref[HALO:, :] = u
        hal_ref[:, cols] = tail
        ulast_ref[0, :, cols] = tail

    def conv(j, win_ref):
        for rc in range(tm // CONV_ROWS):
            for lc in range(tn // CONV_LANES):
                lanes = slice(lc * CONV_LANES, (lc + 1) * CONV_LANES)
                out_lanes = pl.ds(pl.multiple_of(j * tn + lc * CONV_LANES, CONV_LANES), CONV_LANES)
                win = win_ref[rc * CONV_ROWS:(rc + 1) * CONV_ROWS + HALO, lanes]
                c_ref[rc * CONV_ROWS:(rc + 1) * CONV_ROWS, out_lanes] = (
                    _conv_taps(win, wdw_ref, out_lanes) + bdw_ref[:, out_lanes])

    def tile_pair(p, carry):
        glu(2 * p + 1, win_b)
        conv(2 * p, win_a)
        glu(2 * p + 2, win_a)
        conv(2 * p + 1, win_b)
        return carry

    glu(0, win_a)
    lax.fori_loop(0, nj // 2 - 1, tile_pair, 0)
    glu(nj - 1, win_b)
    conv(nj - 2, win_a)
    conv(nj - 1, win_b)


def _glu_conv(x, g, wa, wg, b, wdw, bdw, *, seq_len, tm, tn):
    m, d = x.shape
    tiles_per_seq = seq_len // tm
    assert seq_len % tm == 0 and (d // tn) % 2 == 0
    return pl.pallas_call(
        functools.partial(_glu_conv_kernel, tiles_per_seq=tiles_per_seq),
        grid=(m // tm,),
        in_specs=[
            pl.BlockSpec((tm, d), lambda i: (i, 0)),
            _resident((1, d)),
            _resident((d, d)),
            _resident((d, d)),
            _resident((1, 2 * d)),
            _resident((CONV_WIDTH, d)),
            _resident((1, d)),
        ],
        out_specs=[
            pl.BlockSpec((tm, d), lambda i: (i, 0)),
            pl.BlockSpec((1, HALO, d), lambda i: (i // tiles_per_seq, 0, 0)),
        ],
        out_shape=[jax.ShapeDtypeStruct((m, d), F32),
                   jax.ShapeDtypeStruct((m // seq_len, HALO, d), F32)],
        scratch_shapes=[pltpu.VMEM((tm, d), BF16), pltpu.VMEM((HALO + tm, tn), F32),
                        pltpu.VMEM((HALO + tm, tn), F32), pltpu.VMEM((HALO, d), F32)],
        compiler_params=_params("arbitrary"),
        name="glu_conv",
    )(x, g, wa, wg, b, wdw, bdw)


def _ln_pw2_kernel(c_ref, x_ref, lng_ref, lnb_ref, w2_ref, b2_ref, o_ref):
    o_ref[...] = _ln_silu_pw2(c_ref[...], x_ref[...], lng_ref, lnb_ref, w2_ref, b2_ref)


def _ln_pw2(c, x, lng, lnb, w2, b2, *, tm):
    m, d = x.shape
    rows = pl.BlockSpec((tm, d), lambda i: (i, 0))
    return pl.pallas_call(
        _ln_pw2_kernel,
        grid=(m // tm,),
        in_specs=[rows, rows, _resident((1, d)), _resident((1, d)), _resident((d, d)),
                  _resident((1, d))],
        out_specs=rows,
        out_shape=jax.ShapeDtypeStruct((m, d), F32),
        compiler_params=_params("parallel"),
        name="ln_pw2",
    )(c, x, lng, lnb, w2, b2)


def _conv_sample_kernel(full_ref, x_ref, wdw_ref, bdw_ref, lng_ref, lnb_ref, w2_ref, b2_ref,
                        o_ref, c_ref):
    nb, rows, d = full_ref.shape
    s = rows - (CONV_WIDTH - 1)
    for lc in range(d // CONV_LANES):
        lanes = slice(lc * CONV_LANES, (lc + 1) * CONV_LANES)
        acc = jnp.zeros((nb, s, CONV_LANES), F32)
        for k in range(CONV_WIDTH):
            acc = acc + full_ref[:, k:k + s, lanes] * wdw_ref[k:k + 1, lanes]
        c_ref[:, lanes] = acc.reshape(nb * s, CONV_LANES) + bdw_ref[:, lanes]
    o_ref[...] = _ln_silu_pw2(c_ref[...], x_ref[...], lng_ref, lnb_ref, w2_ref, b2_ref)


def _conv_sample(full, x, wdw, bdw, lng, lnb, w2, b2):
    m, d = x.shape
    whole = lambda a: pl.BlockSpec(a.shape, lambda i: (0,) * a.ndim)
    args = (full, x, wdw, bdw, lng, lnb, w2, b2)
    return pl.pallas_call(
        _conv_sample_kernel,
        grid=(1,),
        in_specs=[whole(a) for a in args],
        out_specs=pl.BlockSpec((m, d), lambda i: (0, 0)),
        out_shape=jax.ShapeDtypeStruct((m, d), F32),
        scratch_shapes=[pltpu.VMEM((m, d), F32)],
        compiler_params=_params("arbitrary"),
        name="conv_sample",
    )(*args)


def _mlp_kernel(x_ref, g_ref, wu_ref, wd_ref, fg_ref, o_ref, *rest, final, emit_weights):
    if emit_weights:
        wu_out_ref, wd_out_ref, h_ref = rest
    else:
        (h_ref,) = rest
    f = pl.program_id(1)

    @pl.when(f == 0)
    def _():
        x = x_ref[...]
        h_ref[...] = (x * _rms_scale(x) * g_ref[...]).astype(BF16)
        o_ref[...] = x

    wu = wu_ref[...].astype(BF16)
    wd = wd_ref[...].astype(BF16)
    if emit_weights:
        wu_out_ref[...] = wu
        wd_out_ref[...] = wd
    hid = jnp.dot(h_ref[...], wu, preferred_element_type=F32)
    hid = jnp.square(jnp.maximum(hid, 0.0)).astype(BF16)
    o_ref[...] += jnp.dot(hid, wd, preferred_element_type=F32)

    if final:
        @pl.when(f == pl.num_programs(1) - 1)
        def _():
            y = o_ref[...]
            o_ref[...] = y * _rms_scale(y) * fg_ref[...]


def _mlp(x, g, *, wu, wd, fg, final, tm, tf, layer=None):
    m, d = x.shape
    emit_weights = layer is not None
    if emit_weights:
        assert m == tm, "each weight tile must be visited exactly once"
        dff = wu.shape[2]
        w_specs = [pl.BlockSpec((None, d, tf), lambda i, f: (layer, 0, f)),
                   pl.BlockSpec((None, tf, d), lambda i, f: (layer, f, 0))]
    else:
        dff = wu.shape[1]
        w_specs = [pl.BlockSpec((d, tf), lambda i, f: (0, f)),
                   pl.BlockSpec((tf, d), lambda i, f: (f, 0))]
    out_specs = [pl.BlockSpec((tm, d), lambda i, f: (i, 0))]
    out_shape = [jax.ShapeDtypeStruct((m, d), F32)]
    if emit_weights:
        out_specs += [pl.BlockSpec((d, tf), lambda i, f: (0, f)),
                      pl.BlockSpec((tf, d), lambda i, f: (f, 0))]
        out_shape += [jax.ShapeDtypeStruct((d, dff), BF16), jax.ShapeDtypeStruct((dff, d), BF16)]
    out = pl.pallas_call(
        functools.partial(_mlp_kernel, final=final, emit_weights=emit_weights),
        grid=(m // tm, dff // tf),
        in_specs=[
            pl.BlockSpec((tm, d), lambda i, f: (i, 0)),
            pl.BlockSpec((1, d), lambda i, f: (0, 0)),
            *w_specs,
            pl.BlockSpec((1, d), lambda i, f: (0, 0)),
        ],
        out_specs=out_specs,
        out_shape=out_shape,
        scratch_shapes=[pltpu.VMEM((tm, d), BF16)],
        compiler_params=_params("parallel", "arbitrary", vmem_limit_bytes=MLP_VMEM_LIMIT_BYTES),
        name="mlp_final" if final else "mlp",
    )(x, g, wu, wd, fg)
    return out if emit_weights else out[0]


def _rope_tables(pos):
    half = ROT_DIM // 2
    inv = ROPE_THETA ** (-jnp.arange(half, dtype=F32) / half)
    ang = pos.astype(F32)[:, None] * inv[None, :]
    cos, sin = lax.optimization_barrier((jnp.cos(ang), jnp.sin(ang)))
    ones = jnp.ones((pos.shape[0], HEAD_DIM - ROT_DIM), F32)
    zeros = jnp.zeros_like(ones)
    zh = jnp.zeros_like(sin)
    per_head = lambda parts: jnp.tile(jnp.concatenate(parts, axis=1), (1, LANES // HEAD_DIM))
    return per_head([cos, cos, ones]), per_head([zh, sin, zeros]), per_head([-sin, zh, zeros])


def _rope_block(blk, cos, sin_up, sin_dn):
    half = ROT_DIM // 2
    return (blk * cos + pltpu.roll(blk, half, 1) * sin_up
            + pltpu.roll(blk, LANES - half, 1) * sin_dn)


PROJ_COLS = 1024


def _qkv_kernel(x_ref, gq_ref, gk_ref, w_ref, cos_ref, sup_ref, sdn_ref, q_ref, kv_ref):
    d = x_ref.shape[1]
    k_width = kv_ref.shape[1] // 2
    x = x_ref[...]
    xn = x * _rms_scale(x)
    hq = (xn * gq_ref[...]).astype(BF16)
    hk = (xn * gk_ref[...]).astype(BF16)
    tables = (cos_ref[...], sup_ref[...], sdn_ref[...])

    def project(h, w_col, out_ref, out_col, width, rope_cols):
        res = jnp.dot(h, w_ref[:, w_col:w_col + width], preferred_element_type=F32)
        for c in range(0, rope_cols, LANES):
            out_ref[:, out_col + c:out_col + c + LANES] = _rope_block(res[:, c:c + LANES], *tables)
        if rope_cols < width:
            out_ref[:, out_col + rope_cols:out_col + width] = res[:, rope_cols:]

    for c0 in range(0, d, PROJ_COLS):
        project(hq, c0, q_ref, c0, PROJ_COLS, PROJ_COLS)
    project(hk, d, kv_ref, 0, 2 * k_width, k_width)


def _qkv(x, gq, gk, w, tables, *, tm, table_blocks):
    m, d = x.shape
    n_kv = w.shape[1] - d
    tspec = pl.BlockSpec((tm, LANES), lambda i: (i % table_blocks, 0))
    return pl.pallas_call(
        _qkv_kernel,
        grid=(m // tm,),
        in_specs=[
            pl.BlockSpec((tm, d), lambda i: (i, 0)),
            _resident((1, d)),
            _resident((1, d)),
            _resident(w.shape),
            tspec, tspec, tspec,
        ],
        out_specs=[
            pl.BlockSpec((tm, d), lambda i: (i, 0)),
            pl.BlockSpec((tm, n_kv), lambda i: (i, 0)),
        ],
        out_shape=[jax.ShapeDtypeStruct((m, d), F32), jax.ShapeDtypeStruct((m, n_kv), F32)],
        compiler_params=_params("parallel"),
        name="qkv",
    )(x, gq, gk, w, *tables)


def _qk(q, k):
    return lax.dot_general(q, k, (((1,), (1,)), ((), ())), preferred_element_type=F32)


def _attn_prompt_kernel(q_ref, kvp_ref, kvc_ref, x_ref, wo_ref, sink_ref, o_ref, kv_ref, ot_ref):
    blk = kvp_ref.shape[1]
    n_blk = q_ref.shape[1] // blk
    k_width = kvc_ref.shape[2] // 2
    group = q_ref.shape[2] // k_width
    n_kv_heads = k_width // HEAD_DIM
    kv_ref[0:blk, :] = kvp_ref[0].astype(BF16)
    kv_ref[blk:, :] = kvc_ref[0].astype(BF16)

    key = lax.broadcasted_iota(jnp.int32, (2 * blk, group * blk), 0)
    qry = lax.broadcasted_iota(jnp.int32, (2 * blk, group * blk), 1) % blk
    own = (key >= blk) & (key - blk <= qry)
    bias_inner = jnp.where(((key < blk) & (key > qry)) | own, 0.0, -jnp.inf).astype(F32)
    bias_first = jnp.where(pl.program_id(1) > 0, bias_inner, jnp.where(own, 0.0, -jnp.inf))
    head_of_lane = lax.broadcasted_iota(jnp.int32, (1, group * blk), 1) // blk
    scale = HEAD_DIM ** -0.5

    def q_block(qb, carry):
        row0 = pl.multiple_of(qb * blk, blk)
        bias = jnp.where(qb == 0, bias_first, bias_inner)

        def scores(j):
            kk = kv_ref[pl.ds(row0, 2 * blk), j * HEAD_DIM:(j + 1) * HEAD_DIM]
            qs = jnp.concatenate(
                [q_ref[0, pl.ds(row0, blk), h * HEAD_DIM:(h + 1) * HEAD_DIM]
                 for h in range(j * group, (j + 1) * group)], axis=0)
            return _qk(kk, (qs * scale).astype(BF16)) + bias

        st = scores(0)
        for j in range(n_kv_heads):
            st_next = scores(j + 1) if j + 1 < n_kv_heads else None
            vv = kv_ref[pl.ds(row0, 2 * blk), k_width + j * HEAD_DIM:k_width + (j + 1) * HEAD_DIM]
            sink = jnp.zeros((1, group * blk), F32)
            for g in range(group):
                sink = jnp.where(head_of_lane == g, sink_ref[j * group + g], sink)
            m = jnp.maximum(jnp.max(st, axis=0, keepdims=True), sink)
            p = jnp.exp(st - m)
            denom = jnp.sum(p, axis=0, keepdims=True) + jnp.exp(sink - m)
            ot = lax.dot_general(vv, p.astype(BF16), (((0,), (0,)), ((), ())),
                                 preferred_element_type=F32) * (1.0 / denom)
            for g in range(group):
                h = j * group + g
                ot_ref[h * HEAD_DIM:(h + 1) * HEAD_DIM, pl.ds(row0, blk)] = (
                    ot[:, g * blk:(g + 1) * blk].astype(BF16))
            st = st_next
        return carry

    lax.fori_loop(0, n_blk, q_block, 0)

    proj = lax.dot_general(ot_ref[...], wo_ref[...], (((0,), (0,)), ((), ())),
                           preferred_element_type=F32)
    o_ref[0] = x_ref[0] + proj


def _attn_prompt(q, kv, x, wo, sinks, *, tq):
    b, t, d = x.shape
    n_kv = kv.shape[2]
    blk = WINDOW
    per = tq // blk
    return pl.pallas_call(
        _attn_prompt_kernel,
        grid=(b, t // tq),
        in_specs=[
            pl.BlockSpec((1, tq, d), lambda bi, n: (bi, n, 0)),
            pl.BlockSpec((1, blk, n_kv), lambda bi, n: (bi, jnp.maximum(n * per - 1, 0), 0)),
            pl.BlockSpec((1, tq, n_kv), lambda bi, n: (bi, n, 0)),
            pl.BlockSpec((1, tq, d), lambda bi, n: (bi, n, 0)),
            _resident((d, d)),
            pl.BlockSpec(memory_space=pltpu.SMEM),
        ],
        out_specs=pl.BlockSpec((1, tq, d), lambda bi, n: (bi, n, 0)),
        out_shape=jax.ShapeDtypeStruct((b, t, d), F32),
        scratch_shapes=[pltpu.VMEM((blk + tq, n_kv), BF16), pltpu.VMEM((d, tq), BF16)],
        compiler_params=_params("parallel", "arbitrary"),
        name="attn_prompt",
    )(q, kv, kv, x, wo, sinks)


def _attn_sample_kernel(q_ref, kvn_ref, ck_ref, cv_ref, x_ref, wo_ref, sink_ref,
                        o_ref, wk_ref, wv_ref, ocat_ref):
    gb, s, d = q_ref.shape
    _, n_kv_heads, hd, win = ck_ref.shape
    k_width = n_kv_heads * hd
    group = d // k_width
    step = pl.program_id(0)

    lane = lax.broadcasted_iota(jnp.int32, (1, win), 1)
    is_new = lane >= win - s
    qi = lax.broadcasted_iota(jnp.int32, (group * s, win), 0) % s
    kl = lax.broadcasted_iota(jnp.int32, (group * s, win), 1)
    bias_new = jnp.where(kl + s <= qi + win, 0.0, -jnp.inf).astype(F32)
    bias_old = jnp.where((kl < s) & (kl > qi), 0.0, -jnp.inf).astype(F32)
    head_of_row = lax.broadcasted_iota(jnp.int32, (group * s, 1), 0) // s
    scale = hd ** -0.5
    nt = (((1,), (1,)), ((), ()))

    def one_batch(bb, carry):
        row0 = pl.multiple_of((step * gb + bb) * s, s)
        qb = q_ref[bb]
        kn = kvn_ref[bb]
        pad = jnp.zeros((win - s, k_width), F32)
        kn_t = jnp.concatenate([pad, kn[:, :k_width]], axis=0).T
        vn_t = jnp.concatenate([pad, kn[:, k_width:]], axis=0).T

        def window(old, new_t, j):
            shifted = pltpu.roll(old, win - s, 1)
            return jnp.where(is_new, new_t[j * hd:(j + 1) * hd, :], shifted)

        def scores(j):
            k_old = ck_ref[bb, j]
            k_win = window(k_old, kn_t, j)
            wk_ref[bb, j] = k_win
            qs = jnp.concatenate([qb[:, h * hd:(h + 1) * hd]
                                  for h in range(j * group, (j + 1) * group)], axis=0)
            qs = (qs * scale).astype(BF16)
            s_new = jnp.dot(qs, k_win.astype(BF16), preferred_element_type=F32) + bias_new
            s_old = jnp.dot(qs, k_old.astype(BF16), preferred_element_type=F32) + bias_old
            return s_new, s_old

        sc = scores(0)
        for j in range(n_kv_heads):
            sc_next = scores(j + 1) if j + 1 < n_kv_heads else None
            v_old = cv_ref[bb, j]
            v_win = window(v_old, vn_t, j)
            wv_ref[bb, j] = v_win
            sink = jnp.zeros((group * s, 1), F32)
            for g in range(group):
                sink = jnp.where(head_of_row == g, sink_ref[j * group + g], sink)
            s_new, s_old = sc
            m = jnp.maximum(jnp.maximum(jnp.max(s_new, axis=-1, keepdims=True),
                                        jnp.max(s_old, axis=-1, keepdims=True)), sink)
            p_new = jnp.exp(s_new - m)
            p_old = jnp.exp(s_old - m)
            denom = (jnp.sum(p_new, axis=-1, keepdims=True) + jnp.sum(p_old, axis=-1, keepdims=True)
                     + jnp.exp(sink - m))
            o = (lax.dot_general(p_new.astype(BF16), v_win.astype(BF16), nt,
                                 preferred_element_type=F32)
                 + lax.dot_general(p_old.astype(BF16), v_old.astype(BF16), nt,
                                   preferred_element_type=F32)) * (1.0 / denom)
            for g in range(group):
                h = j * group + g
                ocat_ref[pl.ds(row0, s), h * hd:(h + 1) * hd] = o[g * s:(g + 1) * s]
            sc = sc_next
        return carry

    lax.fori_loop(0, gb, one_batch, 0)

    @pl.when(step == pl.num_programs(0) - 1)
    def _():
        o_ref[...] = x_ref[...] + jnp.dot(ocat_ref[...].astype(BF16), wo_ref[...],
                                          preferred_element_type=F32)


def _attn_sample(q, kvn, ck, cv, x, wo, sinks, *, gb):
    nb, s, d = q.shape
    m = nb * s
    to_lanes, from_lanes = (0, 2, 3, 1), (0, 3, 1, 2)
    ck_t, cv_t = ck.transpose(to_lanes), cv.transpose(to_lanes)
    cache_spec = pl.BlockSpec((gb,) + ck_t.shape[1:], lambda i: (i, 0, 0, 0))
    x, wk_t, wv_t = pl.pallas_call(
        _attn_sample_kernel,
        grid=(nb // gb,),
        in_specs=[
            pl.BlockSpec((gb, s, d), lambda i: (i, 0, 0)),
            pl.BlockSpec((gb, s, kvn.shape[2]), lambda i: (i, 0, 0)),
            cache_spec,
            cache_spec,
            _resident((m, d)),
            _resident((d, d)),
            pl.BlockSpec(memory_space=pltpu.SMEM),
        ],
        out_specs=[pl.BlockSpec((m, d), lambda i: (0, 0)), cache_spec, cache_spec],
        out_shape=[
            jax.ShapeDtypeStruct((m, d), F32),
            jax.ShapeDtypeStruct(ck_t.shape, F32),
            jax.ShapeDtypeStruct(cv_t.shape, F32),
        ],
        scratch_shapes=[pltpu.VMEM((m, d), F32)],
        compiler_params=_params("arbitrary"),
        name="attn_sample",
    )(q, kvn, ck_t, cv_t, x, wo, sinks)
    return x, wk_t.transpose(from_lanes), wv_t.transpose(from_lanes)


def _row(v):
    return v.reshape(1, -1)


def _as_list(out):
    return list(out) if isinstance(out, (list, tuple)) else [out]


def _tile(m, want):
    return want if m % want == 0 else m


def kernel(x_prompt, x_sample, state_conv, cache_k, cache_v, norm_mix, w_pw1, b_pw1, w_dw, b_dw,
           conv_ln_g, conv_ln_b, w_pw2, b_pw2, kv_norm, w_k, w_v, w_q, w_o, sinks, norm_mlp,
           w_up, w_down, final_norm):
    b, t, d = x_prompt.shape
    nb, s, _ = x_sample.shape
    win = cache_k.shape[1]
    depth = norm_mlp.shape[0]
    n_conv = w_pw1.shape[0]
    assert depth == 2 and n_conv == 1 and w_q.shape[0] == 1, "one conv layer then one attention layer"
    assert t % WINDOW == 0 and win == WINDOW

    w_pw2b = w_pw2[0].astype(BF16)
    w_ob = w_o[0].astype(BF16)
    w_qkvb = jnp.concatenate([w_q[0], w_k, w_v], axis=1).astype(BF16)
    ln_pw2_w = (_row(conv_ln_g[0]), _row(conv_ln_b[0]), w_pw2b, _row(b_pw2[0]))

    def tokens(x, *, mix, attend, tables, table_blocks, tm_qkv, mlp_weights):
        m = x.shape[0]
        tm = _tile(m, 1024)
        tf = 512 if "layer" in mlp_weights(0) else 1024
        fg = _row(final_norm)
        x, conv_state = mix(x)
        x, *w0 = _as_list(_mlp(x, _row(norm_mlp[0]), fg=fg, final=False, tm=tm, tf=tf,
                               **mlp_weights(0)))
        q, kv = _qkv(x, _row(norm_mix[1]), _row(kv_norm), w_qkvb, tables, tm=tm_qkv,
                     table_blocks=table_blocks)
        x, extra = attend(q, kv, x)
        y, *w1 = _as_list(_mlp(x, _row(norm_mlp[1]), fg=fg, final=True, tm=tm, tf=tf,
                               **mlp_weights(1)))
        return y, conv_state, kv, extra, (w0, w1)

    def mix_p(x):
        c, u_last = _glu_conv(x, _row(norm_mix[0]), *pw1_b, _row(b_pw1[0]), w_dw[0], _row(b_dw[0]),
                              seq_len=t, tm=512, tn=256)
        return _ln_pw2(c, x, *ln_pw2_w, tm=512), u_last[:, HALO - (CONV_WIDTH - 1):]

    def attend_p(q, kv, x):
        x3 = _attn_prompt(q.reshape(b, t, d), kv.reshape(b, t, -1), x.reshape(b, t, d), w_ob,
                          sinks[0], tq=512)
        return x3.reshape(b * t, d), None

    pos_s = PAST_LEN + jnp.arange(s, dtype=jnp.int32)

    pw1_b = []

    def mix_s(x):
        u, *pw1_halves = _glu(x, _row(norm_mix[0]), w_pw1[0], _row(b_pw1[0]), tn=512)
        pw1_b.extend(pw1_halves)
        full = jnp.concatenate([state_conv[0], u.reshape(nb, s, d)], axis=1)
        x = _conv_sample(full, x, w_dw[0], _row(b_dw[0]), *ln_pw2_w)
        return x, full[:, -(CONV_WIDTH - 1):]

    def attend_s(q, kv, x):
        x, wk, wv = _attn_sample(q.reshape(nb, s, d), kv.reshape(nb, s, -1), cache_k, cache_v,
                                 x, w_ob, sinks[0], gb=8)
        return x, (wk, wv)

    tables_s = tuple(jnp.tile(tb, (nb, 1)) for tb in _rope_tables(pos_s))
    y_s, conv_s_state, _, (win_k_s, win_v_s), mlp_b = tokens(
        x_sample.reshape(nb * s, d), mix=mix_s, attend=attend_s, tables=tables_s,
        table_blocks=1, tm_qkv=nb * s, mlp_weights=lambda l: dict(wu=w_up, wd=w_down, layer=l))

    tm_p = _tile(b * t, 512)
    y_p, conv_p_state, kv_p, _, _ = tokens(
        x_prompt.reshape(b * t, d), mix=mix_p, attend=attend_p,
        tables=_rope_tables(jnp.arange(t, dtype=jnp.int32)), table_blocks=t // tm_p, tm_qkv=tm_p,
        mlp_weights=lambda l: dict(wu=mlp_b[l][0], wd=mlp_b[l][1]))
    kv_p = kv_p.reshape(b, t, -1)[:, t - WINDOW:].reshape(b, WINDOW, 2, N_KV_HEADS, HEAD_DIM)

    return (y_p.reshape(b, t, d), y_s.reshape(nb, s, d), conv_p_state[None], conv_s_state[None],
            kv_p[:, :, 0], kv_p[:, :, 1], win_k_s, win_v_s)
```

```python
import functools

import jax
import jax.numpy as jnp
from jax import lax
from jax.experimental import pallas as pl
from jax.experimental.pallas import tpu as pltpu

F32 = jnp.float32
BF16 = jnp.bfloat16

EPS = 1e-6
HEAD_DIM = 64
N_KV_HEADS = 8
ROT_DIM = HEAD_DIM // 4
ROPE_THETA = 500000.0
WINDOW = 128
CONV_WIDTH = 31
PAST_LEN = 16384

LANES = 128
SUBLANES = 8
VMEM_LIMIT_BYTES = 54 * 1024 * 1024
MLP_VMEM_LIMIT_BYTES = 60 * 1024 * 1024

HALO = -(-(CONV_WIDTH - 1) // SUBLANES) * SUBLANES
TAP0 = HALO - (CONV_WIDTH - 1)


def _params(*semantics, vmem_limit_bytes=VMEM_LIMIT_BYTES):
    return pltpu.CompilerParams(dimension_semantics=semantics, vmem_limit_bytes=vmem_limit_bytes)


def _resident(shape):
    return pl.BlockSpec(shape, lambda *_: (0,) * len(shape), pipeline_mode=pl.Buffered(1))


def _rms_scale(x):
    return lax.rsqrt(jnp.mean(x * x, axis=-1, keepdims=True) + EPS)


def _glu_kernel(x_ref, g_ref, wa_ref, wg_ref, ba_ref, bg_ref, u_ref, wa_out_ref, wg_out_ref, h_ref):
    @pl.when(pl.program_id(0) == 0)
    def _():
        x = x_ref[...]
        h_ref[...] = (x * _rms_scale(x) * g_ref[...]).astype(BF16)

    h = h_ref[...]
    wa = wa_ref[...].astype(BF16)
    wg = wg_ref[...].astype(BF16)
    wa_out_ref[...] = wa
    wg_out_ref[...] = wg
    a = jnp.dot(h, wa, preferred_element_type=F32) + ba_ref[...]
    gate = jnp.dot(h, wg, preferred_element_type=F32) + bg_ref[...]
    u_ref[...] = a * jax.nn.sigmoid(gate)


def _glu(x, g, w, b, *, tn):
    m, d = x.shape
    nj = d // tn
    return pl.pallas_call(
        _glu_kernel,
        grid=(nj,),
        in_specs=[
            pl.BlockSpec((m, d), lambda j: (0, 0)),
            pl.BlockSpec((1, d), lambda j: (0, 0)),
            pl.BlockSpec((d, tn), lambda j: (0, j)),
            pl.BlockSpec((d, tn), lambda j: (0, j + nj)),
            pl.BlockSpec((1, tn), lambda j: (0, j)),
            pl.BlockSpec((1, tn), lambda j: (0, j + nj)),
        ],
        out_specs=[pl.BlockSpec((m, tn), lambda j: (0, j)),
                   pl.BlockSpec((d, tn), lambda j: (0, j)),
                   pl.BlockSpec((d, tn), lambda j: (0, j))],
        out_shape=[jax.ShapeDtypeStruct((m, d), F32), jax.ShapeDtypeStruct((d, d), BF16),
                   jax.ShapeDtypeStruct((d, d), BF16)],
        scratch_shapes=[pltpu.VMEM((m, d), BF16)],
        compiler_params=_params("arbitrary"),
        name="glu",
    )(x, g, w, w, b, b)


def _ln_silu_pw2(c, x, lng_ref, lnb_ref, w2_ref, b2_ref):
    mu = jnp.mean(c, axis=-1, keepdims=True)
    cc = c - mu
    var = jnp.mean(cc * cc, axis=-1, keepdims=True)
    y = cc * lax.rsqrt(var + EPS) * lng_ref[...] + lnb_ref[...]
    y = y * jax.nn.sigmoid(y)
    return x + jnp.dot(y.astype(BF16), w2_ref[...], preferred_element_type=F32) + b2_ref[...]


CONV_ROWS = 64
CONV_LANES = 128


def _conv_taps(win, wdw_ref, lanes):
    acc = jnp.zeros((CONV_ROWS, CONV_LANES), F32)
    for res in range(SUBLANES):
        shifted = win if res == 0 else pltpu.roll(win, CONV_ROWS + HALO - res, 0)
        for k in range(CONV_WIDTH):
            if (TAP0 + k) % SUBLANES == res:
                a = (TAP0 + k) // SUBLANES * SUBLANES
                acc = acc + shifted[a:a + CONV_ROWS] * wdw_ref[k:k + 1, lanes]
    return acc


def _glu_conv_kernel(x_ref, g_ref, wa_ref, wg_ref, b_ref, wdw_ref, bdw_ref, c_ref, ulast_ref,
                     h_ref, win_a, win_b, hal_ref, *, tiles_per_seq):
    tm, d = x_ref.shape
    tn = win_a.shape[1]
    nj = d // tn
    x = x_ref[...]
    h_ref[...] = (x * _rms_scale(x) * g_ref[...]).astype(BF16)
    first = pl.program_id(0) % tiles_per_seq == 0

    @pl.when(pl.program_id(0) == 0)
    def _():
        hal_ref[...] = jnp.zeros_like(hal_ref)

    def glu(j, win_ref):
        cols = pl.ds(pl.multiple_of(j * tn, tn), tn)
        gcols = pl.ds(pl.multiple_of(d + j * tn, tn), tn)
        a = jnp.dot(h_ref[...], wa_ref[:, cols], preferred_element_type=F32) + b_ref[:, cols]
        gate = jnp.dot(h_ref[...], wg_ref[:, cols], preferred_element_type=F32) + b_ref[:, gcols]
        u = a * jax.nn.sigmoid(gate)
        tail = u[tm - HALO:, :]
        win_ref[0:HALO, :] = jnp.where(first, 0.0, hal_ref[:, cols])
        win_ref[HALO:, :] = u
        hal_ref[:, cols] = tail
        ulast_ref[0, :, cols] = tail

    def conv(j, win_ref):
        for rc in range(tm // CONV_ROWS):
            for lc in range(tn // CONV_LANES):
                lanes = slice(lc * CONV_LANES, (lc + 1) * CONV_LANES)
                out_lanes = pl.ds(pl.multiple_of(j * tn + lc * CONV_LANES, CONV_LANES), CONV_LANES)
                win = win_ref[rc * CONV_ROWS:(rc + 1) * CONV_ROWS + HALO, lanes]
                c_ref[rc * CONV_ROWS:(rc + 1) * CONV_ROWS, out_lanes] = (
                    _conv_taps(win, wdw_ref, out_lanes) + bdw_ref[:, out_lanes])

    def tile_pair(p, carry):
        glu(2 * p + 1, win_b)
        conv(2 * p, win_a)
        glu(2 * p + 2, win_a)
        conv(2 * p + 1, win_b)
        return carry

    glu(0, win_a)
    lax.fori_loop(0, nj // 2 - 1, tile_pair, 0)
    glu(nj - 1, win_b)
    conv(nj - 2, win_a)
    conv(nj - 1, win_b)


def _glu_conv(x, g, wa, wg, b, wdw, bdw, *, seq_len, tm, tn):
    m, d = x.shape
    tiles_per_seq = seq_len // tm
    assert seq_len % tm == 0 and (d // tn) % 2 == 0
    return pl.pallas_call(
        functools.partial(_glu_conv_kernel, tiles_per_seq=tiles_per_seq),
        grid=(m // tm,),
        in_specs=[
            pl.BlockSpec((tm, d), lambda i: (i, 0)),
            _resident((1, d)),
            _resident((d, d)),
            _resident((d, d)),
            _resident((1, 2 * d)),
            _resident((CONV_WIDTH, d)),
            _resident((1, d)),
        ],
        out_specs=[
            pl.BlockSpec((tm, d), lambda i: (i, 0)),
            pl.BlockSpec((1, HALO, d), lambda i: (i // tiles_per_seq, 0, 0)),
        ],
        out_shape=[jax.ShapeDtypeStruct((m, d), F32),
                   jax.ShapeDtypeStruct((m // seq_len, HALO, d), F32)],
        scratch_shapes=[pltpu.VMEM((tm, d), BF16), pltpu.VMEM((HALO + tm, tn), F32),
                        pltpu.VMEM((HALO + tm, tn), F32), pltpu.VMEM((HALO, d), F32)],
        compiler_params=_params("arbitrary"),
        name="glu_conv",
    )(x, g, wa, wg, b, wdw, bdw)


def _ln_pw2_kernel(c_ref, x_ref, lng_ref, lnb_ref, w2_ref, b2_ref, o_ref):
    o_ref[...] = _ln_silu_pw2(c_ref[...], x_ref[...], lng_ref, lnb_ref, w2_ref, b2_ref)


def _ln_pw2(c, x, lng, lnb, w2, b2, *, tm):
    m, d = x.shape
    rows = pl.BlockSpec((tm, d), lambda i: (i, 0))
    return pl.pallas_call(
        _ln_pw2_kernel,
        grid=(m // tm,),
        in_specs=[rows, rows, _resident((1, d)), _resident((1, d)), _resident((d, d)),
                  _resident((1, d))],
        out_specs=rows,
        out_shape=jax.ShapeDtypeStruct((m, d), F32),
        compiler_params=_params("parallel"),
        name="ln_pw2",
    )(c, x, lng, lnb, w2, b2)


def _conv_sample_kernel(full_ref, x_ref, wdw_ref, bdw_ref, lng_ref, lnb_ref, w2_ref, b2_ref,
                        o_ref, c_ref):
    nb, rows, d = full_ref.shape
    s = rows - (CONV_WIDTH - 1)
    for lc in range(d // CONV_LANES):
        lanes = slice(lc * CONV_LANES, (lc + 1) * CONV_LANES)
        acc = jnp.zeros((nb, s, CONV_LANES), F32)
        for k in range(CONV_WIDTH):
            acc = acc + full_ref[:, k:k + s, lanes] * wdw_ref[k:k + 1, lanes]
        c_ref[:, lanes] = acc.reshape(nb * s, CONV_LANES) + bdw_ref[:, lanes]
    o_ref[...] = _ln_silu_pw2(c_ref[...], x_ref[...], lng_ref, lnb_ref, w2_ref, b2_ref)


def _conv_sample(full, x, wdw, bdw, lng, lnb, w2, b2):
    m, d = x.shape
    whole = lambda a: pl.BlockSpec(a.shape, lambda i: (0,) * a.ndim)
    args = (full, x, wdw, bdw, lng, lnb, w2, b2)
    return pl.pallas_call(
        _conv_sample_kernel,
        grid=(1,),
        in_specs=[whole(a) for a in args],
        out_specs=pl.BlockSpec((m, d), lambda i: (0, 0)),
        out_shape=jax.ShapeDtypeStruct((m, d), F32),
        scratch_shapes=[pltpu.VMEM((m, d), F32)],
        compiler_params=_params("arbitrary"),
        name="conv_sample",
    )(*args)


def _mlp_kernel(x_ref, g_ref, wu_ref, wd_ref, fg_ref, o_ref, *rest, final, emit_weights):
    if emit_weights:
        wu_out_ref, wd_out_ref, h_ref = rest
    else:
        (h_ref,) = rest
    f = pl.program_id(1)

    @pl.when(f == 0)
    def _():
        x = x_ref[...]
        h_ref[...] = (x * _rms_scale(x) * g_ref[...]).astype(BF16)
        o_ref[...] = x

    wu = wu_ref[...].astype(BF16)
    wd = wd_ref[...].astype(BF16)
    if emit_weights:
        wu_out_ref[...] = wu
        wd_out_ref[...] = wd
    hid = jnp.dot(h_ref[...], wu, preferred_element_type=F32)
    hid = jnp.square(jnp.maximum(hid, 0.0)).astype(BF16)
    o_ref[...] += jnp.dot(hid, wd, preferred_element_type=F32)

    if final:
        @pl.when(f == pl.num_programs(1) - 1)
        def _():
            y = o_ref[...]
            o_ref[...] = y * _rms_scale(y) * fg_ref[...]


def _mlp(x, g, *, wu, wd, fg, final, tm, tf, layer=None):
    m, d = x.shape
    emit_weights = layer is not None
    if emit_weights:
        assert m == tm, "each weight tile must be visited exactly once"
        dff = wu.shape[2]
        w_specs = [pl.BlockSpec((None, d, tf), lambda i, f: (layer, 0, f)),
                   pl.BlockSpec((None, tf, d), lambda i, f: (layer, f, 0))]
    else:
        dff = wu.shape[1]
        w_specs = [pl.BlockSpec((d, tf), lambda i, f: (0, f)),
                   pl.BlockSpec((tf, d), lambda i, f: (f, 0))]
    out_specs = [pl.BlockSpec((tm, d), lambda i, f: (i, 0))]
    out_shape = [jax.ShapeDtypeStruct((m, d), F32)]
    if emit_weights:
        out_specs += [pl.BlockSpec((d, tf), lambda i, f: (0, f)),
                      pl.BlockSpec((tf, d), lambda i, f: (f, 0))]
        out_shape += [jax.ShapeDtypeStruct((d, dff), BF16), jax.ShapeDtypeStruct((dff, d), BF16)]
    out = pl.pallas_call(
        functools.partial(_mlp_kernel, final=final, emit_weights=emit_weights),
        grid=(m // tm, dff // tf),
        in_specs=[
            pl.BlockSpec((tm, d), lambda i, f: (i, 0)),
            pl.BlockSpec((1, d), lambda i, f: (0, 0)),
            *w_specs,
            pl.BlockSpec((1, d), lambda i, f: (0, 0)),
        ],
        out_specs=out_specs,
        out_shape=out_shape,
        scratch_shapes=[pltpu.VMEM((tm, d), BF16)],
        compiler_params=_params("parallel", "arbitrary", vmem_limit_bytes=MLP_VMEM_LIMIT_BYTES),
        name="mlp_final" if final else "mlp",
    )(x, g, wu, wd, fg)
    return out if emit_weights else out[0]


def _rope_tables(pos):
    half = ROT_DIM // 2
    inv = ROPE_THETA ** (-jnp.arange(half, dtype=F32) / half)
    ang = pos.astype(F32)[:, None] * inv[None, :]
    cos, sin = lax.optimization_barrier((jnp.cos(ang), jnp.sin(ang)))
    ones = jnp.ones((pos.shape[0], HEAD_DIM - ROT_DIM), F32)
    zeros = jnp.zeros_like(ones)
    zh = jnp.zeros_like(sin)
    per_head = lambda parts: jnp.tile(jnp.concatenate(parts, axis=1), (1, LANES // HEAD_DIM))
    return per_head([cos, cos, ones]), per_head([zh, sin, zeros]), per_head([-sin, zh, zeros])


def _rope_block(blk, cos, sin_up, sin_dn):
    half = ROT_DIM // 2
    return (blk * cos + pltpu.roll(blk, half, 1) * sin_up
            + pltpu.roll(blk, LANES - half, 1) * sin_dn)


PROJ_COLS = 1024


def _qkv_kernel(x_ref, gq_ref, gk_ref, w_ref, cos_ref, sup_ref, sdn_ref, q_ref, kv_ref):
    d = x_ref.shape[1]
    k_width = kv_ref.shape[1] // 2
    x = x_ref[...]
    xn = x * _rms_scale(x)
    hq = (xn * gq_ref[...]).astype(BF16)
    hk = (xn * gk_ref[...]).astype(BF16)
    tables = (cos_ref[...], sup_ref[...], sdn_ref[...])

    def project(h, w_col, out_ref, out_col, width, rope_cols):
        res = jnp.dot(h, w_ref[:, w_col:w_col + width], preferred_element_type=F32)
        for c in range(0, rope_cols, LANES):
            out_ref[:, out_col + c:out_col + c + LANES] = _rope_block(res[:, c:c + LANES], *tables)
        if rope_cols < width:
            out_ref[:, out_col + rope_cols:out_col + width] = res[:, rope_cols:]

    for c0 in range(0, d, PROJ_COLS):
        project(hq, c0, q_ref, c0, PROJ_COLS, PROJ_COLS)
    project(hk, d, kv_ref, 0, 2 * k_width, k_width)


def _qkv(x, gq, gk, w, tables, *, tm, table_blocks):
    m, d = x.shape
    n_kv = w.shape[1] - d
    tspec = pl.BlockSpec((tm, LANES), lambda i: (i % table_blocks, 0))
    return pl.pallas_call(
        _qkv_kernel,
        grid=(m // tm,),
        in_specs=[
            pl.BlockSpec((tm, d), lambda i: (i, 0)),
            _resident((1, d)),
            _resident((1, d)),
            _resident(w.shape),
            tspec, tspec, tspec,
        ],
        out_specs=[
            pl.BlockSpec((tm, d), lambda i: (i, 0)),
            pl.BlockSpec((tm, n_kv), lambda i: (i, 0)),
        ],
        out_shape=[jax.ShapeDtypeStruct((m, d), F32), jax.ShapeDtypeStruct((m, n_kv), F32)],
        compiler_params=_params("parallel"),
        name="qkv",
    )(x, gq, gk, w, *tables)


def _qk(q, k):
    return lax.dot_general(q, k, (((1,), (1,)), ((), ())), preferred_element_type=F32)


ATTN_Q_UNROLL = 4


def _attn_prompt_kernel(q_ref, kvp_ref, kvc_ref, x_ref, wo_ref, sink_ref, o_ref, kv_ref, ot_ref):
    blk = kvp_ref.shape[1]
    n_blk = q_ref.shape[1] // blk
    k_width = kvc_ref.shape[2] // 2
    group = q_ref.shape[2] // k_width
    n_kv_heads = k_width // HEAD_DIM
    kv_ref[0:blk, :] = kvp_ref[0].astype(BF16)
    kv_ref[blk:, :] = kvc_ref[0].astype(BF16)

    key = lax.broadcasted_iota(jnp.int32, (2 * blk, group * blk), 0)
    qry = lax.broadcasted_iota(jnp.int32, (2 * blk, group * blk), 1) % blk
    own = (key >= blk) & (key - blk <= qry)
    bias_inner = jnp.where(((key < blk) & (key > qry)) | own, 0.0, -jnp.inf).astype(F32)
    bias_first = jnp.where(pl.program_id(1) > 0, bias_inner, jnp.where(own, 0.0, -jnp.inf))
    head_of_lane = lax.broadcasted_iota(jnp.int32, (1, group * blk), 1) // blk
    scale = HEAD_DIM ** -0.5

    def q_blocks(it, carry):
        def rows(ql):
            return pl.multiple_of((it * ATTN_Q_UNROLL + ql) * blk, blk)

        def scores(ql, j):
            row0 = rows(ql)
            bias = jnp.where(it * ATTN_Q_UNROLL + ql == 0, bias_first, bias_inner) if ql == 0 else bias_inner
            kk = kv_ref[pl.ds(row0, 2 * blk), j * HEAD_DIM:(j + 1) * HEAD_DIM]
            qs = jnp.concatenate(
                [q_ref[0, pl.ds(row0, blk), h * HEAD_DIM:(h + 1) * HEAD_DIM]
                 for h in range(j * group, (j + 1) * group)], axis=0)
            return _qk(kk, (qs * scale).astype(BF16)) + bias

        items = [(ql, j) for ql in range(ATTN_Q_UNROLL) for j in range(n_kv_heads)]
        st = scores(*items[0])
        for idx, (ql, j) in enumerate(items):
            st_next = scores(*items[idx + 1]) if idx + 1 < len(items) else None
            row0 = rows(ql)
            vv = kv_ref[pl.ds(row0, 2 * blk), k_width + j * HEAD_DIM:k_width + (j + 1) * HEAD_DIM]
            sink = jnp.zeros((1, group * blk), F32)
            for g in range(group):
                sink = jnp.where(head_of_lane == g, sink_ref[j * group + g], sink)
            m = jnp.maximum(jnp.max(st, axis=0, keepdims=True), sink)
            p = jnp.exp(st - m)
            denom = jnp.sum(p, axis=0, keepdims=True) + jnp.exp(sink - m)
            ot = lax.dot_general(vv, p.astype(BF16), (((0,), (0,)), ((), ())),
                                 preferred_element_type=F32) * (1.0 / denom)
            for g in range(group):
                h = j * group + g
                ot_ref[h * HEAD_DIM:(h + 1) * HEAD_DIM, pl.ds(row0, blk)] = (
                    ot[:, g * blk:(g + 1) * blk].astype(BF16))
            st = st_next
        return carry

    lax.fori_loop(0, n_blk // ATTN_Q_UNROLL, q_blocks, 0)

    proj = lax.dot_general(ot_ref[...], wo_ref[...], (((0,), (0,)), ((), ())),
                           preferred_element_type=F32)
    o_ref[0] = x_ref[0] + proj


def _attn_prompt(q, kv, x, wo, sinks, *, tq):
    b, t, d = x.shape
    n_kv = kv.shape[2]
    blk = WINDOW
    per = tq // blk
    assert per % ATTN_Q_UNROLL == 0
    return pl.pallas_call(
        _attn_prompt_kernel,
        grid=(b, t // tq),
        in_specs=[
            pl.BlockSpec((1, tq, d), lambda bi, n: (bi, n, 0)),
            pl.BlockSpec((1, blk, n_kv), lambda bi, n: (bi, jnp.maximum(n * per - 1, 0), 0)),
            pl.BlockSpec((1, tq, n_kv), lambda bi, n: (bi, n, 0)),
            pl.BlockSpec((1, tq, d), lambda bi, n: (bi, n, 0)),
            _resident((d, d)),
            pl.BlockSpec(memory_space=pltpu.SMEM),
        ],
        out_specs=pl.BlockSpec((1, tq, d), lambda bi, n: (bi, n, 0)),
        out_shape=jax.ShapeDtypeStruct((b, t, d), F32),
        scratch_shapes=[pltpu.VMEM((blk + tq, n_kv), BF16), pltpu.VMEM((d, tq), BF16)],
        compiler_params=_params("parallel", "arbitrary"),
        name="attn_prompt",
    )(q, kv, kv, x, wo, sinks)


def _attn_sample_kernel(q_ref, kvn_ref, ck_ref, cv_ref, x_ref, wo_ref, sink_ref,
                        o_ref, wk_ref, wv_ref, ocat_ref):
    gb, s, d = q_ref.shape
    _, n_kv_heads, hd, win = ck_ref.shape
    k_width = n_kv_heads * hd
    group = d // k_width
    step = pl.program_id(0)

    lane = lax.broadcasted_iota(jnp.int32, (1, win), 1)
    is_new = lane >= win - s
    qi = lax.broadcasted_iota(jnp.int32, (group * s, win), 0) % s
    kl = lax.broadcasted_iota(jnp.int32, (group * s, win), 1)
    bias_new = jnp.where(kl + s <= qi + win, 0.0, -jnp.inf).astype(F32)
    bias_old = jnp.where((kl < s) & (kl > qi), 0.0, -jnp.inf).astype(F32)
    head_of_row = lax.broadcasted_iota(jnp.int32, (group * s, 1), 0) // s
    scale = hd ** -0.5
    nt = (((1,), (1,)), ((), ()))

    def one_batch(bb, carry):
        row0 = pl.multiple_of((step * gb + bb) * s, s)
        qb = q_ref[bb]
        kn = kvn_ref[bb]
        pad = jnp.zeros((win - s, k_width), F32)
        kn_t = jnp.concatenate([pad, kn[:, :k_width]], axis=0).T
        vn_t = jnp.concatenate([pad, kn[:, k_width:]], axis=0).T

        def window(old, new_t, j):
            shifted = pltpu.roll(old, win - s, 1)
            return jnp.where(is_new, new_t[j * hd:(j + 1) * hd, :], shifted)

        def scores(j):
            k_old = ck_ref[bb, j]
            k_win = window(k_old, kn_t, j)
            wk_ref[bb, j] = k_win
            qs = jnp.concatenate([qb[:, h * hd:(h + 1) * hd]
                                  for h in range(j * group, (j + 1) * group)], axis=0)
            qs = (qs * scale).astype(BF16)
            s_new = jnp.dot(qs, k_win.astype(BF16), preferred_element_type=F32) + bias_new
            s_old = jnp.dot(qs, k_old.astype(BF16), preferred_element_type=F32) + bias_old
            return s_new, s_old

        sc = scores(0)
        for j in range(n_kv_heads):
            sc_next = scores(j + 1) if j + 1 < n_kv_heads else None
            v_old = cv_ref[bb, j]
            v_win = window(v_old, vn_t, j)
            wv_ref[bb, j] = v_win
            sink = jnp.zeros((group * s, 1), F32)
            for g in range(group):
                sink = jnp.where(head_of_row == g, sink_ref[j * group + g], sink)
            s_new, s_old = sc
            m = jnp.maximum(jnp.maximum(jnp.max(s_new, axis=-1, keepdims=True),
                                        jnp.max(s_old, axis=-1, keepdims=True)), sink)
            p_new = jnp.exp(s_new - m)
            p_old = jnp.exp(s_old - m)
            denom = (jnp.sum(p_new, axis=-1, keepdims=True) + jnp.sum(p_old, axis=-1, keepdims=True)
                     + jnp.exp(sink - m))
            o = (lax.dot_general(p_new.astype(BF16), v_win.astype(BF16), nt,
                                 preferred_element_type=F32)
                 + lax.dot_general(p_old.astype(BF16), v_old.astype(BF16), nt,
                                   preferred_element_type=F32)) * (1.0 / denom)
            for g in range(group):
                h = j * group + g
                ocat_ref[pl.ds(row0, s), h * hd:(h + 1) * hd] = o[g * s:(g + 1) * s]
            sc = sc_next
        return carry

    lax.fori_loop(0, gb, one_batch, 0)

    @pl.when(step == pl.num_programs(0) - 1)
    def _():
        o_ref[...] = x_ref[...] + jnp.dot(ocat_ref[...].astype(BF16), wo_ref[...],
                                          preferred_element_type=F32)


def _attn_sample(q, kvn, ck, cv, x, wo, sinks, *, gb):
    nb, s, d = q.shape
    m = nb * s
    to_lanes, from_lanes = (0, 2, 3, 1), (0, 3, 1, 2)
    ck_t, cv_t = ck.transpose(to_lanes), cv.transpose(to_lanes)
    cache_spec = pl.BlockSpec((gb,) + ck_t.shape[1:], lambda i: (i, 0, 0, 0))
    x, wk_t, wv_t = pl.pallas_call(
        _attn_sample_kernel,
        grid=(nb // gb,),
        in_specs=[
            pl.BlockSpec((gb, s, d), lambda i: (i, 0, 0)),
            pl.BlockSpec((gb, s, kvn.shape[2]), lambda i: (i, 0, 0)),
            cache_spec,
            cache_spec,
            _resident((m, d)),
            _resident((d, d)),
            pl.BlockSpec(memory_space=pltpu.SMEM),
        ],
        out_specs=[pl.BlockSpec((m, d), lambda i: (0, 0)), cache_spec, cache_spec],
        out_shape=[
            jax.ShapeDtypeStruct((m, d), F32),
            jax.ShapeDtypeStruct(ck_t.shape, F32),
            jax.ShapeDtypeStruct(cv_t.shape, F32),
        ],
        scratch_shapes=[pltpu.VMEM((m, d), F32)],
        compiler_params=_params("arbitrary"),
        name="attn_sample",
    )(q, kvn, ck_t, cv_t, x, wo, sinks)
    return x, wk_t.transpose(from_lanes), wv_t.transpose(from_lanes)


def _row(v):
    return v.reshape(1, -1)


def _as_list(out):
    return list(out) if isinstance(out, (list, tuple)) else [out]


def _tile(m, want):
    return want if m % want == 0 else m


def kernel(x_prompt, x_sample, state_conv, cache_k, cache_v, norm_mix, w_pw1, b_pw1, w_dw, b_dw,
           conv_ln_g, conv_ln_b, w_pw2, b_pw2, kv_norm, w_k, w_v, w_q, w_o, sinks, norm_mlp,
           w_up, w_down, final_norm):
    b, t, d = x_prompt.shape
    nb, s, _ = x_sample.shape
    win = cache_k.shape[1]
    depth = norm_mlp.shape[0]
    n_conv = w_pw1.shape[0]
    assert depth == 2 and n_conv == 1 and w_q.shape[0] == 1, "one conv layer then one attention layer"
    assert t % WINDOW == 0 and win == WINDOW

    w_pw2b = w_pw2[0].astype(BF16)
    w_ob = w_o[0].astype(BF16)
    w_qkvb = jnp.concatenate([w_q[0], w_k, w_v], axis=1).astype(BF16)
    ln_pw2_w = (_row(conv_ln_g[0]), _row(conv_ln_b[0]), w_pw2b, _row(b_pw2[0]))

    def tokens(x, *, mix, attend, tables, table_blocks, tm_qkv, mlp_weights):
        m = x.shape[0]
        tm = _tile(m, 1024)
        tf = 512 if "layer" in mlp_weights(0) else 1024
        fg = _row(final_norm)
        x, conv_state = mix(x)
        x, *w0 = _as_list(_mlp(x, _row(norm_mlp[0]), fg=fg, final=False, tm=tm, tf=tf,
                               **mlp_weights(0)))
        q, kv = _qkv(x, _row(norm_mix[1]), _row(kv_norm), w_qkvb, tables, tm=tm_qkv,
                     table_blocks=table_blocks)
        x, extra = attend(q, kv, x)
        y, *w1 = _as_list(_mlp(x, _row(norm_mlp[1]), fg=fg, final=True, tm=tm, tf=tf,
                               **mlp_weights(1)))
        return y, conv_state, kv, extra, (w0, w1)

    def mix_p(x):
        c, u_last = _glu_conv(x, _row(norm_mix[0]), *pw1_b, _row(b_pw1[0]), w_dw[0], _row(b_dw[0]),
                              seq_len=t, tm=512, tn=256)
        return _ln_pw2(c, x, *ln_pw2_w, tm=512), u_last[:, HALO - (CONV_WIDTH - 1):]

    def attend_p(q, kv, x):
        x3 = _attn_prompt(q.reshape(b, t, d), kv.reshape(b, t, -1), x.reshape(b, t, d), w_ob,
                          sinks[0], tq=512)
        return x3.reshape(b * t, d), None

    pos_s = PAST_LEN + jnp.arange(s, dtype=jnp.int32)

    pw1_b = []

    def mix_s(x):
        u, *pw1_halves = _glu(x, _row(norm_mix[0]), w_pw1[0], _row(b_pw1[0]), tn=512)
        pw1_b.extend(pw1_halves)
        full = jnp.concatenate([state_conv[0], u.reshape(nb, s, d)], axis=1)
        x = _conv_sample(full, x, w_dw[0], _row(b_dw[0]), *ln_pw2_w)
        return x, full[:, -(CONV_WIDTH - 1):]

    def attend_s(q, kv, x):
        x, wk, wv = _attn_sample(q.reshape(nb, s, d), kv.reshape(nb, s, -1), cache_k, cache_v,
                                 x, w_ob, sinks[0], gb=8)
        return x, (wk, wv)

    tables_s = tuple(jnp.tile(tb, (nb, 1)) for tb in _rope_tables(pos_s))
    y_s, conv_s_state, _, (win_k_s, win_v_s), mlp_b = tokens(
        x_sample.reshape(nb * s, d), mix=mix_s, attend=attend_s, tables=tables_s,
        table_blocks=1, tm_qkv=nb * s, mlp_weights=lambda l: dict(wu=w_up, wd=w_down, layer=l))

    tm_p = _tile(b * t, 512)
    y_p, conv_p_state, kv_p, _, _ = tokens(
        x_prompt.reshape(b * t, d), mix=mix_p, attend=attend_p,
        tables=_rope_tables(jnp.arange(t, dtype=jnp.int32)), table_blocks=t // tm_p, tm_qkv=tm_p,
        mlp_weights=lambda l: dict(wu=mlp_b[l][0], wd=mlp_b[l][1]))
    kv_p = kv_p.reshape(b, t, -1)[:, t - WINDOW:].reshape(b, WINDOW, 2, N_KV_HEADS, HEAD_DIM)

    return (y_p.reshape(b, t, d), y_s.reshape(nb, s, d), conv_p_state[None], conv_s_state[None],
            kv_p[:, :, 0], kv_p[:, :, 1], win_k_s, win_v_s)
```

```python
import functools

import jax
import jax.numpy as jnp
from jax import lax
from jax.experimental import pallas as pl
from jax.experimental.pallas import tpu as pltpu

F32 = jnp.float32
BF16 = jnp.bfloat16

EPS = 1e-6
HEAD_DIM = 64
N_KV_HEADS = 8
ROT_DIM = HEAD_DIM // 4
ROPE_THETA = 500000.0
WINDOW = 128
CONV_WIDTH = 31
PAST_LEN = 16384

LANES = 128
SUBLANES = 8
VMEM_LIMIT_BYTES = 54 * 1024 * 1024
MLP_VMEM_LIMIT_BYTES = 60 * 1024 * 1024

MLP_ROWS = 1024
MLP_HIDDEN = 1024
MLP_HIDDEN_F32 = 512
ROW_TILE = 512
GLU_CONV_COLS = 256
GLU_COLS = 512
SAMPLE_BATCH_GROUP = 8

HALO = -(-(CONV_WIDTH - 1) // SUBLANES) * SUBLANES
TAP0 = HALO - (CONV_WIDTH - 1)


def _params(*semantics, vmem_limit_bytes=VMEM_LIMIT_BYTES):
    return pltpu.CompilerParams(dimension_semantics=semantics, vmem_limit_bytes=vmem_limit_bytes)


def _resident(shape):
    return pl.BlockSpec(shape, lambda *_: (0,) * len(shape), pipeline_mode=pl.Buffered(1))


def _rms_scale(x):
    return lax.rsqrt(jnp.mean(x * x, axis=-1, keepdims=True) + EPS)


def _glu_kernel(x_ref, g_ref, wa_ref, wg_ref, ba_ref, bg_ref, u_ref, wa_out_ref, wg_out_ref, h_ref):
    @pl.when(pl.program_id(0) == 0)
    def _():
        x = x_ref[...]
        h_ref[...] = (x * _rms_scale(x) * g_ref[...]).astype(BF16)

    h = h_ref[...]
    wa = wa_ref[...].astype(BF16)
    wg = wg_ref[...].astype(BF16)
    wa_out_ref[...] = wa
    wg_out_ref[...] = wg
    a = jnp.dot(h, wa, preferred_element_type=F32) + ba_ref[...]
    gate = jnp.dot(h, wg, preferred_element_type=F32) + bg_ref[...]
    u_ref[...] = a * jax.nn.sigmoid(gate)


def _glu(x, g, w, b, *, tn):
    m, d = x.shape
    nj = d // tn
    return pl.pallas_call(
        _glu_kernel,
        grid=(nj,),
        in_specs=[
            pl.BlockSpec((m, d), lambda j: (0, 0)),
            pl.BlockSpec((1, d), lambda j: (0, 0)),
            pl.BlockSpec((d, tn), lambda j: (0, j)),
            pl.BlockSpec((d, tn), lambda j: (0, j + nj)),
            pl.BlockSpec((1, tn), lambda j: (0, j)),
            pl.BlockSpec((1, tn), lambda j: (0, j + nj)),
        ],
        out_specs=[pl.BlockSpec((m, tn), lambda j: (0, j)),
                   pl.BlockSpec((d, tn), lambda j: (0, j)),
                   pl.BlockSpec((d, tn), lambda j: (0, j))],
        out_shape=[jax.ShapeDtypeStruct((m, d), F32), jax.ShapeDtypeStruct((d, d), BF16),
                   jax.ShapeDtypeStruct((d, d), BF16)],
        scratch_shapes=[pltpu.VMEM((m, d), BF16)],
        compiler_params=_params("arbitrary"),
        name="glu",
    )(x, g, w, w, b, b)


def _ln_silu_pw2(c, x, lng_ref, lnb_ref, w2_ref, b2_ref):
    mu = jnp.mean(c, axis=-1, keepdims=True)
    cc = c - mu
    var = jnp.mean(cc * cc, axis=-1, keepdims=True)
    y = cc * lax.rsqrt(var + EPS) * lng_ref[...] + lnb_ref[...]
    y = y * jax.nn.sigmoid(y)
    return x + jnp.dot(y.astype(BF16), w2_ref[...], preferred_element_type=F32) + b2_ref[...]


CONV_ROWS = 64
CONV_LANES = 128


def _conv_taps(win, wdw_ref, lanes):
    acc = jnp.zeros((CONV_ROWS, CONV_LANES), F32)
    for res in range(SUBLANES):
        shifted = win if res == 0 else pltpu.roll(win, CONV_ROWS + HALO - res, 0)
        for k in range(CONV_WIDTH):
            if (TAP0 + k) % SUBLANES == res:
                a = (TAP0 + k) // SUBLANES * SUBLANES
                acc = acc + shifted[a:a + CONV_ROWS] * wdw_ref[k:k + 1, lanes]
    return acc


def _glu_conv_kernel(x_ref, g_ref, wa_ref, wg_ref, b_ref, wdw_ref, bdw_ref, c_ref, ulast_ref,
                     h_ref, win_a, win_b, hal_ref, *, tiles_per_seq):
    tm, d = x_ref.shape
    tn = win_a.shape[1]
    nj = d // tn
    x = x_ref[...]
    h_ref[...] = (x * _rms_scale(x) * g_ref[...]).astype(BF16)
    first = pl.program_id(0) % tiles_per_seq == 0

    @pl.when(pl.program_id(0) == 0)
    def _():
        hal_ref[...] = jnp.zeros_like(hal_ref)

    def glu(j, win_ref):
        cols = pl.ds(pl.multiple_of(j * tn, tn), tn)
        gcols = pl.ds(pl.multiple_of(d + j * tn, tn), tn)
        a = jnp.dot(h_ref[...], wa_ref[:, cols], preferred_element_type=F32) + b_ref[:, cols]
        gate = jnp.dot(h_ref[...], wg_ref[:, cols], preferred_element_type=F32) + b_ref[:, gcols]
        u = a * jax.nn.sigmoid(gate)
        tail = u[tm - HALO:, :]
        win_ref[0:HALO, :] = jnp.where(first, 0.0, hal_ref[:, cols])
        win_ref[HALO:, :] = u
        hal_ref[:, cols] = tail
        ulast_ref[0, :, cols] = tail

    def conv(j, win_ref):
        for rc in range(tm // CONV_ROWS):
            for lc in range(tn // CONV_LANES):
                lanes = slice(lc * CONV_LANES, (lc + 1) * CONV_LANES)
                out_lanes = pl.ds(pl.multiple_of(j * tn + lc * CONV_LANES, CONV_LANES), CONV_LANES)
                win = win_ref[rc * CONV_ROWS:(rc + 1) * CONV_ROWS + HALO, lanes]
                c_ref[rc * CONV_ROWS:(rc + 1) * CONV_ROWS, out_lanes] = (
                    _conv_taps(win, wdw_ref, out_lanes) + bdw_ref[:, out_lanes])

    def tile_pair(p, carry):
        glu(2 * p + 1, win_b)
        conv(2 * p, win_a)
        glu(2 * p + 2, win_a)
        conv(2 * p + 1, win_b)
        return carry

    glu(0, win_a)
    lax.fori_loop(0, nj // 2 - 1, tile_pair, 0)
    glu(nj - 1, win_b)
    conv(nj - 2, win_a)
    conv(nj - 1, win_b)


def _glu_conv(x, g, wa, wg, b, wdw, bdw, *, seq_len, tm, tn):
    m, d = x.shape
    tiles_per_seq = seq_len // tm
    assert seq_len % tm == 0 and (d // tn) % 2 == 0
    return pl.pallas_call(
        functools.partial(_glu_conv_kernel, tiles_per_seq=tiles_per_seq),
        grid=(m // tm,),
        in_specs=[
            pl.BlockSpec((tm, d), lambda i: (i, 0)),
            _resident((1, d)),
            _resident((d, d)),
            _resident((d, d)),
            _resident((1, 2 * d)),
            _resident((CONV_WIDTH, d)),
            _resident((1, d)),
        ],
        out_specs=[
            pl.BlockSpec((tm, d), lambda i: (i, 0)),
            pl.BlockSpec((1, HALO, d), lambda i: (i // tiles_per_seq, 0, 0)),
        ],
        out_shape=[jax.ShapeDtypeStruct((m, d), F32),
                   jax.ShapeDtypeStruct((m // seq_len, HALO, d), F32)],
        scratch_shapes=[pltpu.VMEM((tm, d), BF16), pltpu.VMEM((HALO + tm, tn), F32),
                        pltpu.VMEM((HALO + tm, tn), F32), pltpu.VMEM((HALO, d), F32)],
        compiler_params=_params("arbitrary"),
        name="glu_conv",
    )(x, g, wa, wg, b, wdw, bdw)


def _ln_pw2_kernel(c_ref, x_ref, lng_ref, lnb_ref, w2_ref, b2_ref, o_ref):
    o_ref[...] = _ln_silu_pw2(c_ref[...], x_ref[...], lng_ref, lnb_ref, w2_ref, b2_ref)


def _ln_pw2(c, x, lng, lnb, w2, b2, *, tm):
    m, d = x.shape
    rows = pl.BlockSpec((tm, d), lambda i: (i, 0))
    return pl.pallas_call(
        _ln_pw2_kernel,
        grid=(m // tm,),
        in_specs=[rows, rows, _resident((1, d)), _resident((1, d)), _resident((d, d)),
                  _resident((1, d))],
        out_specs=rows,
        out_shape=jax.ShapeDtypeStruct((m, d), F32),
        compiler_params=_params("parallel"),
        name="ln_pw2",
    )(c, x, lng, lnb, w2, b2)


def _conv_sample_kernel(full_ref, x_ref, wdw_ref, bdw_ref, lng_ref, lnb_ref, w2_ref, b2_ref,
                        o_ref, c_ref):
    nb, rows, d = full_ref.shape
    s = rows - (CONV_WIDTH - 1)
    for lc in range(d // CONV_LANES):
        lanes = slice(lc * CONV_LANES, (lc + 1) * CONV_LANES)
        acc = jnp.zeros((nb, s, CONV_LANES), F32)
        for k in range(CONV_WIDTH):
            acc = acc + full_ref[:, k:k + s, lanes] * wdw_ref[k:k + 1, lanes]
        c_ref[:, lanes] = acc.reshape(nb * s, CONV_LANES) + bdw_ref[:, lanes]
    o_ref[...] = _ln_silu_pw2(c_ref[...], x_ref[...], lng_ref, lnb_ref, w2_ref, b2_ref)


def _conv_sample(full, x, wdw, bdw, lng, lnb, w2, b2):
    m, d = x.shape
    whole = lambda a: pl.BlockSpec(a.shape, lambda i: (0,) * a.ndim)
    args = (full, x, wdw, bdw, lng, lnb, w2, b2)
    return pl.pallas_call(
        _conv_sample_kernel,
        grid=(1,),
        in_specs=[whole(a) for a in args],
        out_specs=pl.BlockSpec((m, d), lambda i: (0, 0)),
        out_shape=jax.ShapeDtypeStruct((m, d), F32),
        scratch_shapes=[pltpu.VMEM((m, d), F32)],
        compiler_params=_params("arbitrary"),
        name="conv_sample",
    )(*args)


def _mlp_kernel(x_ref, g_ref, wu_ref, wd_ref, fg_ref, o_ref, *rest, final, emit_weights):
    if emit_weights:
        wu_out_ref, wd_out_ref, h_ref = rest
    else:
        (h_ref,) = rest
    f = pl.program_id(1)

    @pl.when(f == 0)
    def _():
        x = x_ref[...]
        h_ref[...] = (x * _rms_scale(x) * g_ref[...]).astype(BF16)
        o_ref[...] = x

    wu = wu_ref[...].astype(BF16)
    wd = wd_ref[...].astype(BF16)
    if emit_weights:
        wu_out_ref[...] = wu
        wd_out_ref[...] = wd
    hid = jnp.dot(h_ref[...], wu, preferred_element_type=F32)
    hid = jnp.square(jnp.maximum(hid, 0.0)).astype(BF16)
    o_ref[...] += jnp.dot(hid, wd, preferred_element_type=F32)

    if final:
        @pl.when(f == pl.num_programs(1) - 1)
        def _():
            y = o_ref[...]
            o_ref[...] = y * _rms_scale(y) * fg_ref[...]


def _mlp(x, g, *, wu, wd, fg, final, tm, tf, layer=None):
    m, d = x.shape
    emit_weights = layer is not None
    if emit_weights:
        assert m == tm, "each weight tile must be visited exactly once"
        dff = wu.shape[2]
        w_specs = [pl.BlockSpec((None, d, tf), lambda i, f: (layer, 0, f)),
                   pl.BlockSpec((None, tf, d), lambda i, f: (layer, f, 0))]
    else:
        dff = wu.shape[1]
        w_specs = [pl.BlockSpec((d, tf), lambda i, f: (0, f)),
                   pl.BlockSpec((tf, d), lambda i, f: (f, 0))]
    out_specs = [pl.BlockSpec((tm, d), lambda i, f: (i, 0))]
    out_shape = [jax.ShapeDtypeStruct((m, d), F32)]
    if emit_weights:
        out_specs += [pl.BlockSpec((d, tf), lambda i, f: (0, f)),
                      pl.BlockSpec((tf, d), lambda i, f: (f, 0))]
        out_shape += [jax.ShapeDtypeStruct((d, dff), BF16), jax.ShapeDtypeStruct((dff, d), BF16)]
    out = pl.pallas_call(
        functools.partial(_mlp_kernel, final=final, emit_weights=emit_weights),
        grid=(m // tm, dff // tf),
        in_specs=[
            pl.BlockSpec((tm, d), lambda i, f: (i, 0)),
            pl.BlockSpec((1, d), lambda i, f: (0, 0)),
            *w_specs,
            pl.BlockSpec((1, d), lambda i, f: (0, 0)),
        ],
        out_specs=out_specs,
        out_shape=out_shape,
        scratch_shapes=[pltpu.VMEM((tm, d), BF16)],
        compiler_params=_params("parallel", "arbitrary", vmem_limit_bytes=MLP_VMEM_LIMIT_BYTES),
        name="mlp_final" if final else "mlp",
    )(x, g, wu, wd, fg)
    return out if emit_weights else out[0]


def _rope_tables(pos):
    half = ROT_DIM // 2
    inv = ROPE_THETA ** (-jnp.arange(half, dtype=F32) / half)
    ang = pos.astype(F32)[:, None] * inv[None, :]
    cos, sin = lax.optimization_barrier((jnp.cos(ang), jnp.sin(ang)))
    ones = jnp.ones((pos.shape[0], HEAD_DIM - ROT_DIM), F32)
    zeros = jnp.zeros_like(ones)
    zh = jnp.zeros_like(sin)
    per_head = lambda parts: jnp.tile(jnp.concatenate(parts, axis=1), (1, LANES // HEAD_DIM))
    return per_head([cos, cos, ones]), per_head([zh, sin, zeros]), per_head([-sin, zh, zeros])


def _rope_block(blk, cos, sin_up, sin_dn):
    half = ROT_DIM // 2
    return (blk * cos + pltpu.roll(blk, half, 1) * sin_up
            + pltpu.roll(blk, LANES - half, 1) * sin_dn)


PROJ_COLS = 1024


def _qkv_kernel(x_ref, gq_ref, gk_ref, w_ref, cos_ref, sup_ref, sdn_ref, q_ref, kv_ref):
    d = x_ref.shape[1]
    k_width = kv_ref.shape[1] // 2
    x = x_ref[...]
    xn = x * _rms_scale(x)
    hq = (xn * gq_ref[...]).astype(BF16)
    hk = (xn * gk_ref[...]).astype(BF16)
    tables = (cos_ref[...], sup_ref[...], sdn_ref[...])

    def project(h, w_col, out_ref, out_col, width, rope_cols):
        res = jnp.dot(h, w_ref[:, w_col:w_col + width], preferred_element_type=F32)
        for c in range(0, rope_cols, LANES):
            out_ref[:, out_col + c:out_col + c + LANES] = _rope_block(res[:, c:c + LANES], *tables)
        if rope_cols < width:
            out_ref[:, out_col + rope_cols:out_col + width] = res[:, rope_cols:]

    for c0 in range(0, d, PROJ_COLS):
        project(hq, c0, q_ref, c0, PROJ_COLS, PROJ_COLS)
    project(hk, d, kv_ref, 0, 2 * k_width, k_width)


def _qkv(x, gq, gk, w, tables, *, tm, table_blocks):
    m, d = x.shape
    n_kv = w.shape[1] - d
    tspec = pl.BlockSpec((tm, LANES), lambda i: (i % table_blocks, 0))
    return pl.pallas_call(
        _qkv_kernel,
        grid=(m // tm,),
        in_specs=[
            pl.BlockSpec((tm, d), lambda i: (i, 0)),
            _resident((1, d)),
            _resident((1, d)),
            _resident(w.shape),
            tspec, tspec, tspec,
        ],
        out_specs=[
            pl.BlockSpec((tm, d), lambda i: (i, 0)),
            pl.BlockSpec((tm, n_kv), lambda i: (i, 0)),
        ],
        out_shape=[jax.ShapeDtypeStruct((m, d), F32), jax.ShapeDtypeStruct((m, n_kv), F32)],
        compiler_params=_params("parallel"),
        name="qkv",
    )(x, gq, gk, w, *tables)


def _qk(q, k):
    return lax.dot_general(q, k, (((1,), (1,)), ((), ())), preferred_element_type=F32)


ATTN_Q_UNROLL = 4


def _attn_prompt_kernel(q_ref, kvp_ref, kvc_ref, x_ref, wo_ref, sink_ref, o_ref, kv_ref, ot_ref):
    blk = kvp_ref.shape[1]
    n_blk = q_ref.shape[1] // blk
    k_width = kvc_ref.shape[2] // 2
    group = q_ref.shape[2] // k_width
    n_kv_heads = k_width // HEAD_DIM
    kv_ref[0:blk, :] = kvp_ref[0].astype(BF16)
    kv_ref[blk:, :] = kvc_ref[0].astype(BF16)

    key = lax.broadcasted_iota(jnp.int32, (2 * blk, group * blk), 0)
    qry = lax.broadcasted_iota(jnp.int32, (2 * blk, group * blk), 1) % blk
    own = (key >= blk) & (key - blk <= qry)
    bias_inner = jnp.where(((key < blk) & (key > qry)) | own, 0.0, -jnp.inf).astype(F32)
    bias_first = jnp.where(pl.program_id(1) > 0, bias_inner, jnp.where(own, 0.0, -jnp.inf))
    head_of_lane = lax.broadcasted_iota(jnp.int32, (1, group * blk), 1) // blk
    scale = HEAD_DIM ** -0.5

    def q_blocks(it, carry):
        def rows(ql):
            return pl.multiple_of((it * ATTN_Q_UNROLL + ql) * blk, blk)

        def scores(ql, j):
            row0 = rows(ql)
            bias = jnp.where(it * ATTN_Q_UNROLL + ql == 0, bias_first, bias_inner) if ql == 0 else bias_inner
            kk = kv_ref[pl.ds(row0, 2 * blk), j * HEAD_DIM:(j + 1) * HEAD_DIM]
            qs = jnp.concatenate(
                [q_ref[0, pl.ds(row0, blk), h * HEAD_DIM:(h + 1) * HEAD_DIM]
                 for h in range(j * group, (j + 1) * group)], axis=0)
            return _qk(kk, (qs * scale).astype(BF16)) + bias

        items = [(ql, j) for ql in range(ATTN_Q_UNROLL) for j in range(n_kv_heads)]
        st = scores(*items[0])
        for idx, (ql, j) in enumerate(items):
            st_next = scores(*items[idx + 1]) if idx + 1 < len(items) else None
            row0 = rows(ql)
            vv = kv_ref[pl.ds(row0, 2 * blk), k_width + j * HEAD_DIM:k_width + (j + 1) * HEAD_DIM]
            sink = jnp.zeros((1, group * blk), F32)
            for g in range(group):
                sink = jnp.where(head_of_lane == g, sink_ref[j * group + g], sink)
            m = jnp.maximum(jnp.max(st, axis=0, keepdims=True), sink)
            p = jnp.exp(st - m)
            denom = jnp.sum(p, axis=0, keepdims=True) + jnp.exp(sink - m)
            ot = lax.dot_general(vv, p.astype(BF16), (((0,), (0,)), ((), ())),
                                 preferred_element_type=F32) * (1.0 / denom)
            for g in range(group):
                h = j * group + g
                ot_ref[h * HEAD_DIM:(h + 1) * HEAD_DIM, pl.ds(row0, blk)] = (
                    ot[:, g * blk:(g + 1) * blk].astype(BF16))
            st = st_next
        return carry

    lax.fori_loop(0, n_blk // ATTN_Q_UNROLL, q_blocks, 0)

    proj = lax.dot_general(ot_ref[...], wo_ref[...], (((0,), (0,)), ((), ())),
                           preferred_element_type=F32)
    o_ref[0] = x_ref[0] + proj


def _attn_prompt(q, kv, x, wo, sinks, *, tq):
    b, t, d = x.shape
    n_kv = kv.shape[2]
    blk = WINDOW
    per = tq // blk
    assert per % ATTN_Q_UNROLL == 0
    return pl.pallas_call(
        _attn_prompt_kernel,
        grid=(b, t // tq),
        in_specs=[
            pl.BlockSpec((1, tq, d), lambda bi, n: (bi, n, 0)),
            pl.BlockSpec((1, blk, n_kv), lambda bi, n: (bi, jnp.maximum(n * per - 1, 0), 0)),
            pl.BlockSpec((1, tq, n_kv), lambda bi, n: (bi, n, 0)),
            pl.BlockSpec((1, tq, d), lambda bi, n: (bi, n, 0)),
            _resident((d, d)),
            pl.BlockSpec(memory_space=pltpu.SMEM),
        ],
        out_specs=pl.BlockSpec((1, tq, d), lambda bi, n: (bi, n, 0)),
        out_shape=jax.ShapeDtypeStruct((b, t, d), F32),
        scratch_shapes=[pltpu.VMEM((blk + tq, n_kv), BF16), pltpu.VMEM((d, tq), BF16)],
        compiler_params=_params("parallel", "arbitrary"),
        name="attn_prompt",
    )(q, kv, kv, x, wo, sinks)


def _attn_sample_kernel(q_ref, kvn_ref, ck_ref, cv_ref, x_ref, wo_ref, sink_ref,
                        o_ref, wk_ref, wv_ref, ocat_ref):
    gb, s, d = q_ref.shape
    _, n_kv_heads, hd, win = ck_ref.shape
    k_width = n_kv_heads * hd
    group = d // k_width
    step = pl.program_id(0)

    lane = lax.broadcasted_iota(jnp.int32, (1, win), 1)
    is_new = lane >= win - s
    qi = lax.broadcasted_iota(jnp.int32, (group * s, win), 0) % s
    kl = lax.broadcasted_iota(jnp.int32, (group * s, win), 1)
    bias_new = jnp.where(kl + s <= qi + win, 0.0, -jnp.inf).astype(F32)
    bias_old = jnp.where((kl < s) & (kl > qi), 0.0, -jnp.inf).astype(F32)
    head_of_row = lax.broadcasted_iota(jnp.int32, (group * s, 1), 0) // s
    scale = hd ** -0.5
    nt = (((1,), (1,)), ((), ()))

    def one_batch(bb, carry):
        row0 = pl.multiple_of((step * gb + bb) * s, s)
        qb = q_ref[bb]
        kn = kvn_ref[bb]
        pad = jnp.zeros((win - s, k_width), F32)
        kn_t = jnp.concatenate([pad, kn[:, :k_width]], axis=0).T
        vn_t = jnp.concatenate([pad, kn[:, k_width:]], axis=0).T

        def window(old, new_t, j):
            shifted = pltpu.roll(old, win - s, 1)
            return jnp.where(is_new, new_t[j * hd:(j + 1) * hd, :], shifted)

        def scores(j):
            k_old = ck_ref[bb, j]
            k_win = window(k_old, kn_t, j)
            wk_ref[bb, j] = k_win
            qs = jnp.concatenate([qb[:, h * hd:(h + 1) * hd]
                                  for h in range(j * group, (j + 1) * group)], axis=0)
            qs = (qs * scale).astype(BF16)
            s_new = jnp.dot(qs, k_win.astype(BF16), preferred_element_type=F32) + bias_new
            s_old = jnp.dot(qs, k_old.astype(BF16), preferred_element_type=F32) + bias_old
            return s_new, s_old

        sc = scores(0)
        for j in range(n_kv_heads):
            sc_next = scores(j + 1) if j + 1 < n_kv_heads else None
            v_old = cv_ref[bb, j]
            v_win = window(v_old, vn_t, j)
            wv_ref[bb, j] = v_win
            sink = jnp.zeros((group * s, 1), F32)
            for g in range(group):
                sink = jnp.where(head_of_row == g, sink_ref[j * group + g], sink)
            s_new, s_old = sc
            m = jnp.maximum(jnp.maximum(jnp.max(s_new, axis=-1, keepdims=True),
                                        jnp.max(s_old, axis=-1, keepdims=True)), sink)
            p_new = jnp.exp(s_new - m)
            p_old = jnp.exp(s_old - m)
            denom = (jnp.sum(p_new, axis=-1, keepdims=True) + jnp.sum(p_old, axis=-1, keepdims=True)
                     + jnp.exp(sink - m))
            o = (lax.dot_general(p_new.astype(BF16), v_win.astype(BF16), nt,
                                 preferred_element_type=F32)
                 + lax.dot_general(p_old.astype(BF16), v_old.astype(BF16), nt,
                                   preferred_element_type=F32)) * (1.0 / denom)
            for g in range(group):
                h = j * group + g
                ocat_ref[pl.ds(row0, s), h * hd:(h + 1) * hd] = o[g * s:(g + 1) * s]
            sc = sc_next
        return carry

    lax.fori_loop(0, gb, one_batch, 0)

    @pl.when(step == pl.num_programs(0) - 1)
    def _():
        o_ref[...] = x_ref[...] + jnp.dot(ocat_ref[...].astype(BF16), wo_ref[...],
                                          preferred_element_type=F32)


def _attn_sample(q, kvn, ck, cv, x, wo, sinks, *, gb):
    nb, s, d = q.shape
    m = nb * s
    to_lanes, from_lanes = (0, 2, 3, 1), (0, 3, 1, 2)
    ck_t, cv_t = ck.transpose(to_lanes), cv.transpose(to_lanes)
    cache_spec = pl.BlockSpec((gb,) + ck_t.shape[1:], lambda i: (i, 0, 0, 0))
    x, wk_t, wv_t = pl.pallas_call(
        _attn_sample_kernel,
        grid=(nb // gb,),
        in_specs=[
            pl.BlockSpec((gb, s, d), lambda i: (i, 0, 0)),
            pl.BlockSpec((gb, s, kvn.shape[2]), lambda i: (i, 0, 0)),
            cache_spec,
            cache_spec,
            _resident((m, d)),
            _resident((d, d)),
            pl.BlockSpec(memory_space=pltpu.SMEM),
        ],
        out_specs=[pl.BlockSpec((m, d), lambda i: (0, 0)), cache_spec, cache_spec],
        out_shape=[
            jax.ShapeDtypeStruct((m, d), F32),
            jax.ShapeDtypeStruct(ck_t.shape, F32),
            jax.ShapeDtypeStruct(cv_t.shape, F32),
        ],
        scratch_shapes=[pltpu.VMEM((m, d), F32)],
        compiler_params=_params("arbitrary"),
        name="attn_sample",
    )(q, kvn, ck_t, cv_t, x, wo, sinks)
    return x, wk_t.transpose(from_lanes), wv_t.transpose(from_lanes)


def _row(v):
    return v.reshape(1, -1)


def _as_list(out):
    return list(out) if isinstance(out, (list, tuple)) else [out]


def _tile(m, want):
    return want if m % want == 0 else m


def kernel(x_prompt, x_sample, state_conv, cache_k, cache_v, norm_mix, w_pw1, b_pw1, w_dw, b_dw,
           conv_ln_g, conv_ln_b, w_pw2, b_pw2, kv_norm, w_k, w_v, w_q, w_o, sinks, norm_mlp,
           w_up, w_down, final_norm):
    b, t, d = x_prompt.shape
    nb, s, _ = x_sample.shape
    win = cache_k.shape[1]
    depth = norm_mlp.shape[0]
    n_conv = w_pw1.shape[0]
    assert depth == 2 and n_conv == 1 and w_q.shape[0] == 1, "one conv layer then one attention layer"
    assert t % WINDOW == 0 and win == WINDOW

    w_pw2b = w_pw2[0].astype(BF16)
    w_ob = w_o[0].astype(BF16)
    w_qkvb = jnp.concatenate([w_q[0], w_k, w_v], axis=1).astype(BF16)
    ln_pw2_w = (_row(conv_ln_g[0]), _row(conv_ln_b[0]), w_pw2b, _row(b_pw2[0]))

    def tokens(x, *, mix, attend, tables, table_blocks, tm_qkv, mlp_weights):
        m = x.shape[0]
        tm = _tile(m, MLP_ROWS)
        tf = MLP_HIDDEN_F32 if "layer" in mlp_weights(0) else MLP_HIDDEN
        fg = _row(final_norm)
        x, conv_state = mix(x)
        x, *w0 = _as_list(_mlp(x, _row(norm_mlp[0]), fg=fg, final=False, tm=tm, tf=tf,
                               **mlp_weights(0)))
        q, kv = _qkv(x, _row(norm_mix[1]), _row(kv_norm), w_qkvb, tables, tm=tm_qkv,
                     table_blocks=table_blocks)
        x, extra = attend(q, kv, x)
        y, *w1 = _as_list(_mlp(x, _row(norm_mlp[1]), fg=fg, final=True, tm=tm, tf=tf,
                               **mlp_weights(1)))
        return y, conv_state, kv, extra, (w0, w1)

    def mix_p(x):
        c, u_last = _glu_conv(x, _row(norm_mix[0]), *pw1_b, _row(b_pw1[0]), w_dw[0], _row(b_dw[0]),
                              seq_len=t, tm=ROW_TILE, tn=GLU_CONV_COLS)
        return _ln_pw2(c, x, *ln_pw2_w, tm=ROW_TILE), u_last[:, HALO - (CONV_WIDTH - 1):]

    def attend_p(q, kv, x):
        x3 = _attn_prompt(q.reshape(b, t, d), kv.reshape(b, t, -1), x.reshape(b, t, d), w_ob,
                          sinks[0], tq=ROW_TILE)
        return x3.reshape(b * t, d), None

    pos_s = PAST_LEN + jnp.arange(s, dtype=jnp.int32)

    pw1_b = []

    def mix_s(x):
        u, *pw1_halves = _glu(x, _row(norm_mix[0]), w_pw1[0], _row(b_pw1[0]), tn=GLU_COLS)
        pw1_b.extend(pw1_halves)
        full = jnp.concatenate([state_conv[0], u.reshape(nb, s, d)], axis=1)
        x = _conv_sample(full, x, w_dw[0], _row(b_dw[0]), *ln_pw2_w)
        return x, full[:, -(CONV_WIDTH - 1):]

    def attend_s(q, kv, x):
        x, wk, wv = _attn_sample(q.reshape(nb, s, d), kv.reshape(nb, s, -1), cache_k, cache_v,
                                 x, w_ob, sinks[0], gb=SAMPLE_BATCH_GROUP)
        return x, (wk, wv)

    tables_s = tuple(jnp.tile(tb, (nb, 1)) for tb in _rope_tables(pos_s))
    y_s, conv_s_state, _, (win_k_s, win_v_s), mlp_b = tokens(
        x_sample.reshape(nb * s, d), mix=mix_s, attend=attend_s, tables=tables_s,
        table_blocks=1, tm_qkv=nb * s, mlp_weights=lambda l: dict(wu=w_up, wd=w_down, layer=l))

    tm_p = _tile(b * t, ROW_TILE)
    y_p, conv_p_state, kv_p, _, _ = tokens(
        x_prompt.reshape(b * t, d), mix=mix_p, attend=attend_p,
        tables=_rope_tables(jnp.arange(t, dtype=jnp.int32)), table_blocks=t // tm_p, tm_qkv=tm_p,
        mlp_weights=lambda l: dict(wu=mlp_b[l][0], wd=mlp_b[l][1]))
    kv_p = kv_p.reshape(b, t, -1)[:, t - WINDOW:].reshape(b, WINDOW, 2, N_KV_HEADS, HEAD_DIM)

    return (y_p.reshape(b, t, d), y_s.reshape(nb, s, d), conv_p_state[None], conv_s_state[None],
            kv_p[:, :, 0], kv_p[:, :, 1], win_k_s, win_v_s)
```

```python
import functools

import jax
import jax.numpy as jnp
from jax import lax
from jax.experimental import pallas as pl
from jax.experimental.pallas import tpu as pltpu

F32 = jnp.float32
BF16 = jnp.bfloat16

EPS = 1e-6
HEAD_DIM = 64
N_KV_HEADS = 8
ROT_DIM = HEAD_DIM // 4
ROPE_THETA = 500000.0
WINDOW = 128
CONV_WIDTH = 31
PAST_LEN = 16384

LANES = 128
SUBLANES = 8
VMEM_LIMIT_BYTES = 54 * 1024 * 1024
MLP_VMEM_LIMIT_BYTES = 60 * 1024 * 1024

MLP_ROWS = 1024
MLP_HIDDEN = 1024
MLP_HIDDEN_F32 = 512
ROW_TILE = 512
GLU_CONV_COLS = 256
GLU_COLS = 512
SAMPLE_BATCH_GROUP = 8

HALO = -(-(CONV_WIDTH - 1) // SUBLANES) * SUBLANES
TAP0 = HALO - (CONV_WIDTH - 1)


def _params(*semantics, vmem_limit_bytes=VMEM_LIMIT_BYTES):
    return pltpu.CompilerParams(dimension_semantics=semantics, vmem_limit_bytes=vmem_limit_bytes)


def _resident(shape):
    return pl.BlockSpec(shape, lambda *_: (0,) * len(shape), pipeline_mode=pl.Buffered(1))


def _rms_scale(x):
    return lax.rsqrt(jnp.mean(x * x, axis=-1, keepdims=True) + EPS)


def _glu_kernel(x_ref, g_ref, wa_ref, wg_ref, ba_ref, bg_ref, u_ref, wa_out_ref, wg_out_ref, h_ref):
    @pl.when(pl.program_id(0) == 0)
    def _():
        x = x_ref[...]
        h_ref[...] = (x * _rms_scale(x) * g_ref[...]).astype(BF16)

    h = h_ref[...]
    wa = wa_ref[...].astype(BF16)
    wg = wg_ref[...].astype(BF16)
    wa_out_ref[...] = wa
    wg_out_ref[...] = wg
    a = jnp.dot(h, wa, preferred_element_type=F32) + ba_ref[...]
    gate = jnp.dot(h, wg, preferred_element_type=F32) + bg_ref[...]
    u_ref[...] = a * jax.nn.sigmoid(gate)


def _glu(x, g, w, b, *, tn):
    m, d = x.shape
    nj = d // tn
    return pl.pallas_call(
        _glu_kernel,
        grid=(nj,),
        in_specs=[
            pl.BlockSpec((m, d), lambda j: (0, 0)),
            pl.BlockSpec((1, d), lambda j: (0, 0)),
            pl.BlockSpec((d, tn), lambda j: (0, j)),
            pl.BlockSpec((d, tn), lambda j: (0, j + nj)),
            pl.BlockSpec((1, tn), lambda j: (0, j)),
            pl.BlockSpec((1, tn), lambda j: (0, j + nj)),
        ],
        out_specs=[pl.BlockSpec((m, tn), lambda j: (0, j)),
                   pl.BlockSpec((d, tn), lambda j: (0, j)),
                   pl.BlockSpec((d, tn), lambda j: (0, j))],
        out_shape=[jax.ShapeDtypeStruct((m, d), F32), jax.ShapeDtypeStruct((d, d), BF16),
                   jax.ShapeDtypeStruct((d, d), BF16)],
        scratch_shapes=[pltpu.VMEM((m, d), BF16)],
        compiler_params=_params("arbitrary"),
        name="glu",
    )(x, g, w, w, b, b)


def _ln_silu_pw2(c, x, lng_ref, lnb_ref, w2_ref, b2_ref):
    mu = jnp.mean(c, axis=-1, keepdims=True)
    cc = c - mu
    var = jnp.mean(cc * cc, axis=-1, keepdims=True)
    y = cc * lax.rsqrt(var + EPS) * lng_ref[...] + lnb_ref[...]
    y = y * jax.nn.sigmoid(y)
    return x + jnp.dot(y.astype(BF16), w2_ref[...], preferred_element_type=F32) + b2_ref[...]


CONV_ROWS = 64
CONV_LANES = 128


def _conv_taps(win, wdw_ref, lanes):
    acc = jnp.zeros((CONV_ROWS, CONV_LANES), F32)
    for res in range(SUBLANES):
        shifted = win if res == 0 else pltpu.roll(win, CONV_ROWS + HALO - res, 0)
        for k in range(CONV_WIDTH):
            if (TAP0 + k) % SUBLANES == res:
                a = (TAP0 + k) // SUBLANES * SUBLANES
                acc = acc + shifted[a:a + CONV_ROWS] * wdw_ref[k:k + 1, lanes]
    return acc


def _glu_conv_kernel(x_ref, g_ref, wa_ref, wg_ref, b_ref, wdw_ref, bdw_ref, c_ref, ulast_ref,
                     h_ref, win_a, win_b, hal_ref, *, tiles_per_seq):
    tm, d = x_ref.shape
    tn = win_a.shape[1]
    nj = d // tn
    x = x_ref[...]
    h_ref[...] = (x * _rms_scale(x) * g_ref[...]).astype(BF16)
    first = pl.program_id(0) % tiles_per_seq == 0

    @pl.when(pl.program_id(0) == 0)
    def _():
        hal_ref[...] = jnp.zeros_like(hal_ref)

    def glu(j, win_ref):
        cols = pl.ds(pl.multiple_of(j * tn, tn), tn)
        gcols = pl.ds(pl.multiple_of(d + j * tn, tn), tn)
        a = jnp.dot(h_ref[...], wa_ref[:, cols], preferred_element_type=F32) + b_ref[:, cols]
        gate = jnp.dot(h_ref[...], wg_ref[:, cols], preferred_element_type=F32) + b_ref[:, gcols]
        u = a * jax.nn.sigmoid(gate)
        tail = u[tm - HALO:, :]
        win_ref[0:HALO, :] = jnp.where(first, 0.0, hal_ref[:, cols])
        win_ref[HALO:, :] = u
        hal_ref[:, cols] = tail
        ulast_ref[0, :, cols] = tail

    def conv(j, win_ref):
        for rc in range(tm // CONV_ROWS):
            for lc in range(tn // CONV_LANES):
                lanes = slice(lc * CONV_LANES, (lc + 1) * CONV_LANES)
                out_lanes = pl.ds(pl.multiple_of(j * tn + lc * CONV_LANES, CONV_LANES), CONV_LANES)
                win = win_ref[rc * CONV_ROWS:(rc + 1) * CONV_ROWS + HALO, lanes]
                c_ref[rc * CONV_ROWS:(rc + 1) * CONV_ROWS, out_lanes] = (
                    _conv_taps(win, wdw_ref, out_lanes) + bdw_ref[:, out_lanes])

    def tile_pair(p, carry):
        glu(2 * p + 1, win_b)
        conv(2 * p, win_a)
        glu(2 * p + 2, win_a)
        conv(2 * p + 1, win_b)
        return carry

    glu(0, win_a)
    lax.fori_loop(0, nj // 2 - 1, tile_pair, 0)
    glu(nj - 1, win_b)
    conv(nj - 2, win_a)
    conv(nj - 1, win_b)


def _glu_conv(x, g, wa, wg, b, wdw, bdw, *, seq_len, tm, tn):
    m, d = x.shape
    tiles_per_seq = seq_len // tm
    assert seq_len % tm == 0 and (d // tn) % 2 == 0
    return pl.pallas_call(
        functools.partial(_glu_conv_kernel, tiles_per_seq=tiles_per_seq),
        grid=(m // tm,),
        in_specs=[
            pl.BlockSpec((tm, d), lambda i: (i, 0)),
            _resident((1, d)),
            _resident((d, d)),
            _resident((d, d)),
            _resident((1, 2 * d)),
            _resident((CONV_WIDTH, d)),
            _resident((1, d)),
        ],
        out_specs=[
            pl.BlockSpec((tm, d), lambda i: (i, 0)),
            pl.BlockSpec((1, HALO, d), lambda i: (i // tiles_per_seq, 0, 0)),
        ],
        out_shape=[jax.ShapeDtypeStruct((m, d), F32),
                   jax.ShapeDtypeStruct((m // seq_len, HALO, d), F32)],
        scratch_shapes=[pltpu.VMEM((tm, d), BF16), pltpu.VMEM((HALO + tm, tn), F32),
                        pltpu.VMEM((HALO + tm, tn), F32), pltpu.VMEM((HALO, d), F32)],
        compiler_params=_params("arbitrary"),
        name="glu_conv",
    )(x, g, wa, wg, b, wdw, bdw)


def _ln_pw2_kernel(c_ref, x_ref, lng_ref, lnb_ref, w2_ref, b2_ref, o_ref):
    o_ref[...] = _ln_silu_pw2(c_ref[...], x_ref[...], lng_ref, lnb_ref, w2_ref, b2_ref)


def _ln_pw2(c, x, lng, lnb, w2, b2, *, tm):
    m, d = x.shape
    rows = pl.BlockSpec((tm, d), lambda i: (i, 0))
    return pl.pallas_call(
        _ln_pw2_kernel,
        grid=(m // tm,),
        in_specs=[rows, rows, _resident((1, d)), _resident((1, d)), _resident((d, d)),
                  _resident((1, d))],
        out_specs=rows,
        out_shape=jax.ShapeDtypeStruct((m, d), F32),
        compiler_params=_params("parallel"),
        name="ln_pw2",
    )(c, x, lng, lnb, w2, b2)


def _conv_sample_kernel(full_ref, x_ref, wdw_ref, bdw_ref, lng_ref, lnb_ref, w2_ref, b2_ref,
                        o_ref, c_ref):
    nb, rows, d = full_ref.shape
    s = rows - (CONV_WIDTH - 1)
    for lc in range(d // CONV_LANES):
        lanes = slice(lc * CONV_LANES, (lc + 1) * CONV_LANES)
        acc = jnp.zeros((nb, s, CONV_LANES), F32)
        for k in range(CONV_WIDTH):
            acc = acc + full_ref[:, k:k + s, lanes] * wdw_ref[k:k + 1, lanes]
        c_ref[:, lanes] = acc.reshape(nb * s, CONV_LANES) + bdw_ref[:, lanes]
    o_ref[...] = _ln_silu_pw2(c_ref[...], x_ref[...], lng_ref, lnb_ref, w2_ref, b2_ref)


def _conv_sample(full, x, wdw, bdw, lng, lnb, w2, b2):
    m, d = x.shape
    whole = lambda a: pl.BlockSpec(a.shape, lambda i: (0,) * a.ndim)
    args = (full, x, wdw, bdw, lng, lnb, w2, b2)
    return pl.pallas_call(
        _conv_sample_kernel,
        grid=(1,),
        in_specs=[whole(a) for a in args],
        out_specs=pl.BlockSpec((m, d), lambda i: (0, 0)),
        out_shape=jax.ShapeDtypeStruct((m, d), F32),
        scratch_shapes=[pltpu.VMEM((m, d), F32)],
        compiler_params=_params("arbitrary"),
        name="conv_sample",
    )(*args)


def _mlp_kernel(x_ref, g_ref, wu_ref, wd_ref, fg_ref, o_ref, *rest, final, emit_weights):
    if emit_weights:
        wu_out_ref, wd_out_ref, h_ref = rest
    else:
        (h_ref,) = rest
    f = pl.program_id(1)

    @pl.when(f == 0)
    def _():
        x = x_ref[...]
        h_ref[...] = (x * _rms_scale(x) * g_ref[...]).astype(BF16)
        o_ref[...] = x

    wu = wu_ref[...].astype(BF16)
    wd = wd_ref[...].astype(BF16)
    if emit_weights:
        wu_out_ref[...] = wu
        wd_out_ref[...] = wd
    hid = jnp.dot(h_ref[...], wu, preferred_element_type=F32)
    hid = jnp.square(jnp.maximum(hid, 0.0)).astype(BF16)
    o_ref[...] += jnp.dot(hid, wd, preferred_element_type=F32)

    if final:
        @pl.when(f == pl.num_programs(1) - 1)
        def _():
            y = o_ref[...]
            o_ref[...] = y * _rms_scale(y) * fg_ref[...]


def _mlp(x, g, *, wu, wd, fg, final, tm, tf, layer=None):
    m, d = x.shape
    emit_weights = layer is not None
    if emit_weights:
        assert m == tm, "each weight tile must be visited exactly once"
        dff = wu.shape[2]
        w_specs = [pl.BlockSpec((None, d, tf), lambda i, f: (layer, 0, f)),
                   pl.BlockSpec((None, tf, d), lambda i, f: (layer, f, 0))]
    else:
        dff = wu.shape[1]
        w_specs = [pl.BlockSpec((d, tf), lambda i, f: (0, f)),
                   pl.BlockSpec((tf, d), lambda i, f: (f, 0))]
    out_specs = [pl.BlockSpec((tm, d), lambda i, f: (i, 0))]
    out_shape = [jax.ShapeDtypeStruct((m, d), F32)]
    if emit_weights:
        out_specs += [pl.BlockSpec((d, tf), lambda i, f: (0, f)),
                      pl.BlockSpec((tf, d), lambda i, f: (f, 0))]
        out_shape += [jax.ShapeDtypeStruct((d, dff), BF16), jax.ShapeDtypeStruct((dff, d), BF16)]
    out = pl.pallas_call(
        functools.partial(_mlp_kernel, final=final, emit_weights=emit_weights),
        grid=(m // tm, dff // tf),
        in_specs=[
            pl.BlockSpec((tm, d), lambda i, f: (i, 0)),
            pl.BlockSpec((1, d), lambda i, f: (0, 0)),
            *w_specs,
            pl.BlockSpec((1, d), lambda i, f: (0, 0)),
        ],
        out_specs=out_specs,
        out_shape=out_shape,
        scratch_shapes=[pltpu.VMEM((tm, d), BF16)],
        compiler_params=_params("parallel", "arbitrary", vmem_limit_bytes=MLP_VMEM_LIMIT_BYTES),
        name="mlp_final" if final else "mlp",
    )(x, g, wu, wd, fg)
    return out if emit_weights else out[0]


def _rope_tables(pos):
    half = ROT_DIM // 2
    inv = ROPE_THETA ** (-jnp.arange(half, dtype=F32) / half)
    ang = pos.astype(F32)[:, None] * inv[None, :]
    cos, sin = lax.optimization_barrier((jnp.cos(ang), jnp.sin(ang)))
    ones = jnp.ones((pos.shape[0], HEAD_DIM - ROT_DIM), F32)
    zeros = jnp.zeros_like(ones)
    zh = jnp.zeros_like(sin)
    per_head = lambda parts: jnp.tile(jnp.concatenate(parts, axis=1), (1, LANES // HEAD_DIM))
    return per_head([cos, cos, ones]), per_head([zh, sin, zeros]), per_head([-sin, zh, zeros])


def _rope_block(blk, cos, sin_up, sin_dn):
    half = ROT_DIM // 2
    return (blk * cos + pltpu.roll(blk, half, 1) * sin_up
            + pltpu.roll(blk, LANES - half, 1) * sin_dn)


PROJ_COLS = 1024


def _qkv_kernel(x_ref, gq_ref, gk_ref, w_ref, cos_ref, sup_ref, sdn_ref, q_ref, kv_ref):
    d = x_ref.shape[1]
    k_width = kv_ref.shape[1] // 2
    x = x_ref[...]
    xn = x * _rms_scale(x)
    hq = (xn * gq_ref[...]).astype(BF16)
    hk = (xn * gk_ref[...]).astype(BF16)
    tables = (cos_ref[...], sup_ref[...], sdn_ref[...])

    def project(h, w_col, out_ref, out_col, width, rope_cols, scale=None):
        res = jnp.dot(h, w_ref[:, w_col:w_col + width], preferred_element_type=F32)
        for c in range(0, rope_cols, LANES):
            blk = _rope_block(res[:, c:c + LANES], *tables)
            blk = blk if scale is None else blk * scale
            out_ref[:, out_col + c:out_col + c + LANES] = blk.astype(out_ref.dtype)
        if rope_cols < width:
            out_ref[:, out_col + rope_cols:out_col + width] = res[:, rope_cols:]

    for c0 in range(0, d, PROJ_COLS):
        project(hq, c0, q_ref, c0, PROJ_COLS, PROJ_COLS, scale=HEAD_DIM ** -0.5)
    project(hk, d, kv_ref, 0, 2 * k_width, k_width)


def _qkv(x, gq, gk, w, tables, *, tm, table_blocks, q_dtype):
    m, d = x.shape
    n_kv = w.shape[1] - d
    tspec = pl.BlockSpec((tm, LANES), lambda i: (i % table_blocks, 0))
    return pl.pallas_call(
        _qkv_kernel,
        grid=(m // tm,),
        in_specs=[
            pl.BlockSpec((tm, d), lambda i: (i, 0)),
            _resident((1, d)),
            _resident((1, d)),
            _resident(w.shape),
            tspec, tspec, tspec,
        ],
        out_specs=[
            pl.BlockSpec((tm, d), lambda i: (i, 0)),
            pl.BlockSpec((tm, n_kv), lambda i: (i, 0)),
        ],
        out_shape=[jax.ShapeDtypeStruct((m, d), q_dtype), jax.ShapeDtypeStruct((m, n_kv), F32)],
        compiler_params=_params("parallel"),
        name="qkv",
    )(x, gq, gk, w, *tables)


def _qk(q, k):
    return lax.dot_general(q, k, (((1,), (1,)), ((), ())), preferred_element_type=F32)


ATTN_Q_UNROLL = 4


def _attn_prompt_kernel(q_ref, kvp_ref, kvc_ref, x_ref, wo_ref, sink_ref, o_ref, kv_ref, ot_ref):
    blk = kvp_ref.shape[1]
    n_blk = q_ref.shape[1] // blk
    k_width = kvc_ref.shape[2] // 2
    group = q_ref.shape[2] // k_width
    n_kv_heads = k_width // HEAD_DIM
    kv_ref[0:blk, :] = kvp_ref[0].astype(BF16)
    kv_ref[blk:, :] = kvc_ref[0].astype(BF16)

    key = lax.broadcasted_iota(jnp.int32, (2 * blk, group * blk), 0)
    qry = lax.broadcasted_iota(jnp.int32, (2 * blk, group * blk), 1) % blk
    own = (key >= blk) & (key - blk <= qry)
    bias_inner = jnp.where(((key < blk) & (key > qry)) | own, 0.0, -jnp.inf).astype(F32)
    bias_first = jnp.where(pl.program_id(1) > 0, bias_inner, jnp.where(own, 0.0, -jnp.inf))
    head_of_lane = lax.broadcasted_iota(jnp.int32, (1, group * blk), 1) // blk

    def q_blocks(it, carry):
        def rows(ql):
            return pl.multiple_of((it * ATTN_Q_UNROLL + ql) * blk, blk)

        def scores(ql, j):
            row0 = rows(ql)
            bias = jnp.where(it * ATTN_Q_UNROLL + ql == 0, bias_first, bias_inner) if ql == 0 else bias_inner
            kk = kv_ref[pl.ds(row0, 2 * blk), j * HEAD_DIM:(j + 1) * HEAD_DIM]
            qs = jnp.concatenate(
                [q_ref[0, pl.ds(row0, blk), h * HEAD_DIM:(h + 1) * HEAD_DIM]
                 for h in range(j * group, (j + 1) * group)], axis=0)
            return _qk(kk, qs) + bias

        items = [(ql, j) for ql in range(ATTN_Q_UNROLL) for j in range(n_kv_heads)]
        st = scores(*items[0])
        for idx, (ql, j) in enumerate(items):
            st_next = scores(*items[idx + 1]) if idx + 1 < len(items) else None
            row0 = rows(ql)
            vv = kv_ref[pl.ds(row0, 2 * blk), k_width + j * HEAD_DIM:k_width + (j + 1) * HEAD_DIM]
            sink = jnp.zeros((1, group * blk), F32)
            for g in range(group):
                sink = jnp.where(head_of_lane == g, sink_ref[j * group + g], sink)
            m = jnp.maximum(jnp.max(st, axis=0, keepdims=True), sink)
            p = jnp.exp(st - m)
            denom = jnp.sum(p, axis=0, keepdims=True) + jnp.exp(sink - m)
            ot = lax.dot_general(vv, p.astype(BF16), (((0,), (0,)), ((), ())),
                                 preferred_element_type=F32) * (1.0 / denom)
            for g in range(group):
                h = j * group + g
                ot_ref[h * HEAD_DIM:(h + 1) * HEAD_DIM, pl.ds(row0, blk)] = (
                    ot[:, g * blk:(g + 1) * blk].astype(BF16))
            st = st_next
        return carry

    lax.fori_loop(0, n_blk // ATTN_Q_UNROLL, q_blocks, 0)

    proj = lax.dot_general(ot_ref[...], wo_ref[...], (((0,), (0,)), ((), ())),
                           preferred_element_type=F32)
    o_ref[0] = x_ref[0] + proj


def _attn_prompt(q, kv, x, wo, sinks, *, tq):
    b, t, d = x.shape
    n_kv = kv.shape[2]
    blk = WINDOW
    per = tq // blk
    assert per % ATTN_Q_UNROLL == 0
    return pl.pallas_call(
        _attn_prompt_kernel,
        grid=(b, t // tq),
        in_specs=[
            pl.BlockSpec((1, tq, d), lambda bi, n: (bi, n, 0)),
            pl.BlockSpec((1, blk, n_kv), lambda bi, n: (bi, jnp.maximum(n * per - 1, 0), 0)),
            pl.BlockSpec((1, tq, n_kv), lambda bi, n: (bi, n, 0)),
            pl.BlockSpec((1, tq, d), lambda bi, n: (bi, n, 0)),
            _resident((d, d)),
            pl.BlockSpec(memory_space=pltpu.SMEM),
        ],
        out_specs=pl.BlockSpec((1, tq, d), lambda bi, n: (bi, n, 0)),
        out_shape=jax.ShapeDtypeStruct((b, t, d), F32),
        scratch_shapes=[pltpu.VMEM((blk + tq, n_kv), BF16), pltpu.VMEM((d, tq), BF16)],
        compiler_params=_params("parallel", "arbitrary"),
        name="attn_prompt",
    )(q, kv, kv, x, wo, sinks)


def _attn_sample_kernel(q_ref, kvn_ref, ck_ref, cv_ref, x_ref, wo_ref, sink_ref,
                        o_ref, wk_ref, wv_ref, ocat_ref):
    gb, s, d = q_ref.shape
    _, n_kv_heads, hd, win = ck_ref.shape
    k_width = n_kv_heads * hd
    group = d // k_width
    step = pl.program_id(0)

    lane = lax.broadcasted_iota(jnp.int32, (1, win), 1)
    is_new = lane >= win - s
    qi = lax.broadcasted_iota(jnp.int32, (group * s, win), 0) % s
    kl = lax.broadcasted_iota(jnp.int32, (group * s, win), 1)
    bias_new = jnp.where(kl + s <= qi + win, 0.0, -jnp.inf).astype(F32)
    bias_old = jnp.where((kl < s) & (kl > qi), 0.0, -jnp.inf).astype(F32)
    head_of_row = lax.broadcasted_iota(jnp.int32, (group * s, 1), 0) // s
    nt = (((1,), (1,)), ((), ()))

    def one_batch(bb, carry):
        row0 = pl.multiple_of((step * gb + bb) * s, s)
        qb = q_ref[bb]
        kn = kvn_ref[bb]
        pad = jnp.zeros((win - s, k_width), F32)
        kn_t = jnp.concatenate([pad, kn[:, :k_width]], axis=0).T
        vn_t = jnp.concatenate([pad, kn[:, k_width:]], axis=0).T

        def window(old, new_t, j):
            shifted = pltpu.roll(old, win - s, 1)
            return jnp.where(is_new, new_t[j * hd:(j + 1) * hd, :], shifted)

        def scores(j):
            k_old = ck_ref[bb, j]
            k_win = window(k_old, kn_t, j)
            wk_ref[bb, j] = k_win
            qs = jnp.concatenate([qb[:, h * hd:(h + 1) * hd]
                                  for h in range(j * group, (j + 1) * group)], axis=0)
            qs = qs.astype(BF16)
            s_new = jnp.dot(qs, k_win.astype(BF16), preferred_element_type=F32) + bias_new
            s_old = jnp.dot(qs, k_old.astype(BF16), preferred_element_type=F32) + bias_old
            return s_new, s_old

        sc = scores(0)
        for j in range(n_kv_heads):
            sc_next = scores(j + 1) if j + 1 < n_kv_heads else None
            v_old = cv_ref[bb, j]
            v_win = window(v_old, vn_t, j)
            wv_ref[bb, j] = v_win
            sink = jnp.zeros((group * s, 1), F32)
            for g in range(group):
                sink = jnp.where(head_of_row == g, sink_ref[j * group + g], sink)
            s_new, s_old = sc
            m = jnp.maximum(jnp.maximum(jnp.max(s_new, axis=-1, keepdims=True),
                                        jnp.max(s_old, axis=-1, keepdims=True)), sink)
            p_new = jnp.exp(s_new - m)
            p_old = jnp.exp(s_old - m)
            denom = (jnp.sum(p_new, axis=-1, keepdims=True) + jnp.sum(p_old, axis=-1, keepdims=True)
                     + jnp.exp(sink - m))
            o = (lax.dot_general(p_new.astype(BF16), v_win.astype(BF16), nt,
                                 preferred_element_type=F32)
                 + lax.dot_general(p_old.astype(BF16), v_old.astype(BF16), nt,
                                   preferred_element_type=F32)) * (1.0 / denom)
            for g in range(group):
                h = j * group + g
                ocat_ref[pl.ds(row0, s), h * hd:(h + 1) * hd] = o[g * s:(g + 1) * s]
            sc = sc_next
        return carry

    lax.fori_loop(0, gb, one_batch, 0)

    @pl.when(step == pl.num_programs(0) - 1)
    def _():
        o_ref[...] = x_ref[...] + jnp.dot(ocat_ref[...].astype(BF16), wo_ref[...],
                                          preferred_element_type=F32)


def _attn_sample(q, kvn, ck, cv, x, wo, sinks, *, gb):
    nb, s, d = q.shape
    m = nb * s
    to_lanes, from_lanes = (0, 2, 3, 1), (0, 3, 1, 2)
    ck_t, cv_t = ck.transpose(to_lanes), cv.transpose(to_lanes)
    cache_spec = pl.BlockSpec((gb,) + ck_t.shape[1:], lambda i: (i, 0, 0, 0))
    x, wk_t, wv_t = pl.pallas_call(
        _attn_sample_kernel,
        grid=(nb // gb,),
        in_specs=[
            pl.BlockSpec((gb, s, d), lambda i: (i, 0, 0)),
            pl.BlockSpec((gb, s, kvn.shape[2]), lambda i: (i, 0, 0)),
            cache_spec,
            cache_spec,
            _resident((m, d)),
            _resident((d, d)),
            pl.BlockSpec(memory_space=pltpu.SMEM),
        ],
        out_specs=[pl.BlockSpec((m, d), lambda i: (0, 0)), cache_spec, cache_spec],
        out_shape=[
            jax.ShapeDtypeStruct((m, d), F32),
            jax.ShapeDtypeStruct(ck_t.shape, F32),
            jax.ShapeDtypeStruct(cv_t.shape, F32),
        ],
        scratch_shapes=[pltpu.VMEM((m, d), F32)],
        compiler_params=_params("arbitrary"),
        name="attn_sample",
    )(q, kvn, ck_t, cv_t, x, wo, sinks)
    return x, wk_t.transpose(from_lanes), wv_t.transpose(from_lanes)


def _row(v):
    return v.reshape(1, -1)


def _as_list(out):
    return list(out) if isinstance(out, (list, tuple)) else [out]


def _tile(m, want):
    return want if m % want == 0 else m


def kernel(x_prompt, x_sample, state_conv, cache_k, cache_v, norm_mix, w_pw1, b_pw1, w_dw, b_dw,
           conv_ln_g, conv_ln_b, w_pw2, b_pw2, kv_norm, w_k, w_v, w_q, w_o, sinks, norm_mlp,
           w_up, w_down, final_norm):
    b, t, d = x_prompt.shape
    nb, s, _ = x_sample.shape
    win = cache_k.shape[1]
    depth = norm_mlp.shape[0]
    n_conv = w_pw1.shape[0]
    assert depth == 2 and n_conv == 1 and w_q.shape[0] == 1, "one conv layer then one attention layer"
    assert t % WINDOW == 0 and win == WINDOW

    w_pw2b = w_pw2[0].astype(BF16)
    w_ob = w_o[0].astype(BF16)
    w_qkvb = jnp.concatenate([w_q[0], w_k, w_v], axis=1).astype(BF16)
    ln_pw2_w = (_row(conv_ln_g[0]), _row(conv_ln_b[0]), w_pw2b, _row(b_pw2[0]))

    def tokens(x, *, mix, attend, tables, table_blocks, tm_qkv, q_dtype, mlp_weights):
        m = x.shape[0]
        tm = _tile(m, MLP_ROWS)
        tf = MLP_HIDDEN_F32 if "layer" in mlp_weights(0) else MLP_HIDDEN
        fg = _row(final_norm)
        x, conv_state = mix(x)
        x, *w0 = _as_list(_mlp(x, _row(norm_mlp[0]), fg=fg, final=False, tm=tm, tf=tf,
                               **mlp_weights(0)))
        q, kv = _qkv(x, _row(norm_mix[1]), _row(kv_norm), w_qkvb, tables, tm=tm_qkv,
                     table_blocks=table_blocks, q_dtype=q_dtype)
        x, extra = attend(q, kv, x)
        y, *w1 = _as_list(_mlp(x, _row(norm_mlp[1]), fg=fg, final=True, tm=tm, tf=tf,
                               **mlp_weights(1)))
        return y, conv_state, kv, extra, (w0, w1)

    def mix_p(x):
        c, u_last = _glu_conv(x, _row(norm_mix[0]), *pw1_b, _row(b_pw1[0]), w_dw[0], _row(b_dw[0]),
                              seq_len=t, tm=ROW_TILE, tn=GLU_CONV_COLS)
        return _ln_pw2(c, x, *ln_pw2_w, tm=ROW_TILE), u_last[:, HALO - (CONV_WIDTH - 1):]

    def attend_p(q, kv, x):
        x3 = _attn_prompt(q.reshape(b, t, d), kv.reshape(b, t, -1), x.reshape(b, t, d), w_ob,
                          sinks[0], tq=ROW_TILE)
        return x3.reshape(b * t, d), None

    pos_s = PAST_LEN + jnp.arange(s, dtype=jnp.int32)

    pw1_b = []

    def mix_s(x):
        u, *pw1_halves = _glu(x, _row(norm_mix[0]), w_pw1[0], _row(b_pw1[0]), tn=GLU_COLS)
        pw1_b.extend(pw1_halves)
        full = jnp.concatenate([state_conv[0], u.reshape(nb, s, d)], axis=1)
        x = _conv_sample(full, x, w_dw[0], _row(b_dw[0]), *ln_pw2_w)
        return x, full[:, -(CONV_WIDTH - 1):]

    def attend_s(q, kv, x):
        x, wk, wv = _attn_sample(q.reshape(nb, s, d), kv.reshape(nb, s, -1), cache_k, cache_v,
                                 x, w_ob, sinks[0], gb=SAMPLE_BATCH_GROUP)
        return x, (wk, wv)

    tables_s = tuple(jnp.tile(tb, (nb, 1)) for tb in _rope_tables(pos_s))
    y_s, conv_s_state, _, (win_k_s, win_v_s), mlp_b = tokens(
        x_sample.reshape(nb * s, d), mix=mix_s, attend=attend_s, tables=tables_s,
        table_blocks=1, tm_qkv=nb * s, q_dtype=F32,
        mlp_weights=lambda l: dict(wu=w_up, wd=w_down, layer=l))

    tm_p = _tile(b * t, ROW_TILE)
    y_p, conv_p_state, kv_p, _, _ = tokens(
        x_prompt.reshape(b * t, d), mix=mix_p, attend=attend_p,
        tables=_rope_tables(jnp.arange(t, dtype=jnp.int32)), table_blocks=t // tm_p, tm_qkv=tm_p,
        q_dtype=BF16,
        mlp_weights=lambda l: dict(wu=mlp_b[l][0], wd=mlp_b[l][1]))
    kv_p = kv_p.reshape(b, t, -1)[:, t - WINDOW:].reshape(b, WINDOW, 2, N_KV_HEADS, HEAD_DIM)

    return (y_p.reshape(b, t, d), y_s.reshape(nb, s, d), conv_p_state[None], conv_s_state[None],
            kv_p[:, :, 0], kv_p[:, :, 1], win_k_s, win_v_s)
```

```python
import functools

import jax
import jax.numpy as jnp
from jax import lax
from jax.experimental import pallas as pl
from jax.experimental.pallas import tpu as pltpu

F32 = jnp.float32
BF16 = jnp.bfloat16

EPS = 1e-6
HEAD_DIM = 64
N_KV_HEADS = 8
ROT_DIM = HEAD_DIM // 4
ROPE_THETA = 500000.0
WINDOW = 128
CONV_WIDTH = 31
PAST_LEN = 16384

LANES = 128
SUBLANES = 8
VMEM_LIMIT_BYTES = 54 * 1024 * 1024
MLP_VMEM_LIMIT_BYTES = 60 * 1024 * 1024

MLP_ROWS = 1024
MLP_HIDDEN = 1024
MLP_HIDDEN_F32 = 1024
ROW_TILE = 512
GLU_CONV_COLS = 256
GLU_COLS = 512
SAMPLE_BATCH_GROUP = 8

HALO = -(-(CONV_WIDTH - 1) // SUBLANES) * SUBLANES
TAP0 = HALO - (CONV_WIDTH - 1)


def _params(*semantics, vmem_limit_bytes=VMEM_LIMIT_BYTES):
    return pltpu.CompilerParams(dimension_semantics=semantics, vmem_limit_bytes=vmem_limit_bytes)


def _resident(shape):
    return pl.BlockSpec(shape, lambda *_: (0,) * len(shape), pipeline_mode=pl.Buffered(1))


def _rms_scale(x):
    return lax.rsqrt(jnp.mean(x * x, axis=-1, keepdims=True) + EPS)


def _glu_kernel(x_ref, g_ref, wa_ref, wg_ref, ba_ref, bg_ref, u_ref, wa_out_ref, wg_out_ref, h_ref):
    @pl.when(pl.program_id(0) == 0)
    def _():
        x = x_ref[...]
        h_ref[...] = (x * _rms_scale(x) * g_ref[...]).astype(BF16)

    h = h_ref[...]
    wa = wa_ref[...].astype(BF16)
    wg = wg_ref[...].astype(BF16)
    wa_out_ref[...] = wa
    wg_out_ref[...] = wg
    a = jnp.dot(h, wa, preferred_element_type=F32) + ba_ref[...]
    gate = jnp.dot(h, wg, preferred_element_type=F32) + bg_ref[...]
    u_ref[...] = a * jax.nn.sigmoid(gate)


def _glu(x, g, w, b, *, tn):
    m, d = x.shape
    nj = d // tn
    return pl.pallas_call(
        _glu_kernel,
        grid=(nj,),
        in_specs=[
            pl.BlockSpec((m, d), lambda j: (0, 0)),
            pl.BlockSpec((1, d), lambda j: (0, 0)),
            pl.BlockSpec((d, tn), lambda j: (0, j)),
            pl.BlockSpec((d, tn), lambda j: (0, j + nj)),
            pl.BlockSpec((1, tn), lambda j: (0, j)),
            pl.BlockSpec((1, tn), lambda j: (0, j + nj)),
        ],
        out_specs=[pl.BlockSpec((m, tn), lambda j: (0, j)),
                   pl.BlockSpec((d, tn), lambda j: (0, j)),
                   pl.BlockSpec((d, tn), lambda j: (0, j))],
        out_shape=[jax.ShapeDtypeStruct((m, d), F32), jax.ShapeDtypeStruct((d, d), BF16),
                   jax.ShapeDtypeStruct((d, d), BF16)],
        scratch_shapes=[pltpu.VMEM((m, d), BF16)],
        compiler_params=_params("arbitrary"),
        name="glu",
    )(x, g, w, w, b, b)


def _ln_silu_pw2(c, x, lng_ref, lnb_ref, w2_ref, b2_ref):
    mu = jnp.mean(c, axis=-1, keepdims=True)
    cc = c - mu
    var = jnp.mean(cc * cc, axis=-1, keepdims=True)
    y = cc * lax.rsqrt(var + EPS) * lng_ref[...] + lnb_ref[...]
    y = y * jax.nn.sigmoid(y)
    return x + jnp.dot(y.astype(BF16), w2_ref[...], preferred_element_type=F32) + b2_ref[...]


CONV_ROWS = 64
CONV_LANES = 128


def _conv_taps(win, wdw_ref, lanes):
    acc = jnp.zeros((CONV_ROWS, CONV_LANES), F32)
    for res in range(SUBLANES):
        shifted = win if res == 0 else pltpu.roll(win, CONV_ROWS + HALO - res, 0)
        for k in range(CONV_WIDTH):
            if (TAP0 + k) % SUBLANES == res:
                a = (TAP0 + k) // SUBLANES * SUBLANES
                acc = acc + shifted[a:a + CONV_ROWS] * wdw_ref[k:k + 1, lanes]
    return acc


def _glu_conv_kernel(x_ref, g_ref, wa_ref, wg_ref, b_ref, wdw_ref, bdw_ref, c_ref, ulast_ref,
                     h_ref, win_a, win_b, hal_ref, *, tiles_per_seq):
    tm, d = x_ref.shape
    tn = win_a.shape[1]
    nj = d // tn
    x = x_ref[...]
    h_ref[...] = (x * _rms_scale(x) * g_ref[...]).astype(BF16)
    first = pl.program_id(0) % tiles_per_seq == 0

    @pl.when(pl.program_id(0) == 0)
    def _():
        hal_ref[...] = jnp.zeros_like(hal_ref)

    def glu(j, win_ref):
        cols = pl.ds(pl.multiple_of(j * tn, tn), tn)
        gcols = pl.ds(pl.multiple_of(d + j * tn, tn), tn)
        a = jnp.dot(h_ref[...], wa_ref[:, cols], preferred_element_type=F32) + b_ref[:, cols]
        gate = jnp.dot(h_ref[...], wg_ref[:, cols], preferred_element_type=F32) + b_ref[:, gcols]
        u = a * jax.nn.sigmoid(gate)
        tail = u[tm - HALO:, :]
        win_ref[0:HALO, :] = jnp.where(first, 0.0, hal_ref[:, cols])
        win_ref[HALO:, :] = u
        hal_ref[:, cols] = tail
        ulast_ref[0, :, cols] = tail

    def conv(j, win_ref):
        for rc in range(tm // CONV_ROWS):
            for lc in range(tn // CONV_LANES):
                lanes = slice(lc * CONV_LANES, (lc + 1) * CONV_LANES)
                out_lanes = pl.ds(pl.multiple_of(j * tn + lc * CONV_LANES, CONV_LANES), CONV_LANES)
                win = win_ref[rc * CONV_ROWS:(rc + 1) * CONV_ROWS + HALO, lanes]
                c_ref[rc * CONV_ROWS:(rc + 1) * CONV_ROWS, out_lanes] = (
                    _conv_taps(win, wdw_ref, out_lanes) + bdw_ref[:, out_lanes])

    def tile_pair(p, carry):
        glu(2 * p + 1, win_b)
        conv(2 * p, win_a)
        glu(2 * p + 2, win_a)
        conv(2 * p + 1, win_b)
        return carry

    glu(0, win_a)
    lax.fori_loop(0, nj // 2 - 1, tile_pair, 0)
    glu(nj - 1, win_b)
    conv(nj - 2, win_a)
    conv(nj - 1, win_b)


def _glu_conv(x, g, wa, wg, b, wdw, bdw, *, seq_len, tm, tn):
    m, d = x.shape
    tiles_per_seq = seq_len // tm
    assert seq_len % tm == 0 and (d // tn) % 2 == 0
    return pl.pallas_call(
        functools.partial(_glu_conv_kernel, tiles_per_seq=tiles_per_seq),
        grid=(m // tm,),
        in_specs=[
            pl.BlockSpec((tm, d), lambda i: (i, 0)),
            _resident((1, d)),
            _resident((d, d)),
            _resident((d, d)),
            _resident((1, 2 * d)),
            _resident((CONV_WIDTH, d)),
            _resident((1, d)),
        ],
        out_specs=[
            pl.BlockSpec((tm, d), lambda i: (i, 0)),
            pl.BlockSpec((1, HALO, d), lambda i: (i // tiles_per_seq, 0, 0)),
        ],
        out_shape=[jax.ShapeDtypeStruct((m, d), F32),
                   jax.ShapeDtypeStruct((m // seq_len, HALO, d), F32)],
        scratch_shapes=[pltpu.VMEM((tm, d), BF16), pltpu.VMEM((HALO + tm, tn), F32),
                        pltpu.VMEM((HALO + tm, tn), F32), pltpu.VMEM((HALO, d), F32)],
        compiler_params=_params("arbitrary"),
        name="glu_conv",
    )(x, g, wa, wg, b, wdw, bdw)


def _ln_pw2_kernel(c_ref, x_ref, lng_ref, lnb_ref, w2_ref, b2_ref, o_ref):
    o_ref[...] = _ln_silu_pw2(c_ref[...], x_ref[...], lng_ref, lnb_ref, w2_ref, b2_ref)


def _ln_pw2(c, x, lng, lnb, w2, b2, *, tm):
    m, d = x.shape
    rows = pl.BlockSpec((tm, d), lambda i: (i, 0))
    return pl.pallas_call(
        _ln_pw2_kernel,
        grid=(m // tm,),
        in_specs=[rows, rows, _resident((1, d)), _resident((1, d)), _resident((d, d)),
                  _resident((1, d))],
        out_specs=rows,
        out_shape=jax.ShapeDtypeStruct((m, d), F32),
        compiler_params=_params("parallel"),
        name="ln_pw2",
    )(c, x, lng, lnb, w2, b2)


def _conv_sample_kernel(full_ref, x_ref, wdw_ref, bdw_ref, lng_ref, lnb_ref, w2_ref, b2_ref,
                        o_ref, c_ref):
    nb, rows, d = full_ref.shape
    s = rows - (CONV_WIDTH - 1)
    for lc in range(d // CONV_LANES):
        lanes = slice(lc * CONV_LANES, (lc + 1) * CONV_LANES)
        acc = jnp.zeros((nb, s, CONV_LANES), F32)
        for k in range(CONV_WIDTH):
            acc = acc + full_ref[:, k:k + s, lanes] * wdw_ref[k:k + 1, lanes]
        c_ref[:, lanes] = acc.reshape(nb * s, CONV_LANES) + bdw_ref[:, lanes]
    o_ref[...] = _ln_silu_pw2(c_ref[...], x_ref[...], lng_ref, lnb_ref, w2_ref, b2_ref)


def _conv_sample(full, x, wdw, bdw, lng, lnb, w2, b2):
    m, d = x.shape
    whole = lambda a: pl.BlockSpec(a.shape, lambda i: (0,) * a.ndim)
    args = (full, x, wdw, bdw, lng, lnb, w2, b2)
    return pl.pallas_call(
        _conv_sample_kernel,
        grid=(1,),
        in_specs=[whole(a) for a in args],
        out_specs=pl.BlockSpec((m, d), lambda i: (0, 0)),
        out_shape=jax.ShapeDtypeStruct((m, d), F32),
        scratch_shapes=[pltpu.VMEM((m, d), F32)],
        compiler_params=_params("arbitrary"),
        name="conv_sample",
    )(*args)


def _mlp_kernel(x_ref, g_ref, wu_ref, wd_ref, fg_ref, o_ref, *rest, final, emit_weights):
    if emit_weights:
        wu_out_ref, wd_out_ref, h_ref = rest
    else:
        (h_ref,) = rest
    f = pl.program_id(1)

    @pl.when(f == 0)
    def _():
        x = x_ref[...]
        h_ref[...] = (x * _rms_scale(x) * g_ref[...]).astype(BF16)
        o_ref[...] = x

    wu = wu_ref[...].astype(BF16)
    wd = wd_ref[...].astype(BF16)
    if emit_weights:
        wu_out_ref[...] = wu
        wd_out_ref[...] = wd
    hid = jnp.dot(h_ref[...], wu, preferred_element_type=F32)
    hid = jnp.square(jnp.maximum(hid, 0.0)).astype(BF16)
    o_ref[...] += jnp.dot(hid, wd, preferred_element_type=F32)

    if final:
        @pl.when(f == pl.num_programs(1) - 1)
        def _():
            y = o_ref[...]
            o_ref[...] = y * _rms_scale(y) * fg_ref[...]


def _mlp(x, g, *, wu, wd, fg, final, tm, tf, layer=None):
    m, d = x.shape
    emit_weights = layer is not None
    if emit_weights:
        assert m == tm, "each weight tile must be visited exactly once"
        dff = wu.shape[2]
        w_specs = [pl.BlockSpec((None, d, tf), lambda i, f: (layer, 0, f)),
                   pl.BlockSpec((None, tf, d), lambda i, f: (layer, f, 0))]
    else:
        dff = wu.shape[1]
        w_specs = [pl.BlockSpec((d, tf), lambda i, f: (0, f)),
                   pl.BlockSpec((tf, d), lambda i, f: (f, 0))]
    out_specs = [pl.BlockSpec((tm, d), lambda i, f: (i, 0))]
    out_shape = [jax.ShapeDtypeStruct((m, d), F32)]
    if emit_weights:
        out_specs += [pl.BlockSpec((d, tf), lambda i, f: (0, f)),
                      pl.BlockSpec((tf, d), lambda i, f: (f, 0))]
        out_shape += [jax.ShapeDtypeStruct((d, dff), BF16), jax.ShapeDtypeStruct((dff, d), BF16)]
    out = pl.pallas_call(
        functools.partial(_mlp_kernel, final=final, emit_weights=emit_weights),
        grid=(m // tm, dff // tf),
        in_specs=[
            pl.BlockSpec((tm, d), lambda i, f: (i, 0)),
            pl.BlockSpec((1, d), lambda i, f: (0, 0)),
            *w_specs,
            pl.BlockSpec((1, d), lambda i, f: (0, 0)),
        ],
        out_specs=out_specs,
        out_shape=out_shape,
        scratch_shapes=[pltpu.VMEM((tm, d), BF16)],
        compiler_params=_params("parallel", "arbitrary", vmem_limit_bytes=MLP_VMEM_LIMIT_BYTES),
        name="mlp_final" if final else "mlp",
    )(x, g, wu, wd, fg)
    return out if emit_weights else out[0]


def _rope_tables(pos):
    half = ROT_DIM // 2
    inv = ROPE_THETA ** (-jnp.arange(half, dtype=F32) / half)
    ang = pos.astype(F32)[:, None] * inv[None, :]
    cos, sin = lax.optimization_barrier((jnp.cos(ang), jnp.sin(ang)))
    ones = jnp.ones((pos.shape[0], HEAD_DIM - ROT_DIM), F32)
    zeros = jnp.zeros_like(ones)
    zh = jnp.zeros_like(sin)
    per_head = lambda parts: jnp.tile(jnp.concatenate(parts, axis=1), (1, LANES // HEAD_DIM))
    return per_head([cos, cos, ones]), per_head([zh, sin, zeros]), per_head([-sin, zh, zeros])


def _rope_block(blk, cos, sin_up, sin_dn):
    half = ROT_DIM // 2
    return (blk * cos + pltpu.roll(blk, half, 1) * sin_up
            + pltpu.roll(blk, LANES - half, 1) * sin_dn)


PROJ_COLS = 1024


def _qkv_kernel(x_ref, gq_ref, gk_ref, w_ref, cos_ref, sup_ref, sdn_ref, q_ref, kv_ref):
    d = x_ref.shape[1]
    k_width = kv_ref.shape[1] // 2
    x = x_ref[...]
    xn = x * _rms_scale(x)
    hq = (xn * gq_ref[...]).astype(BF16)
    hk = (xn * gk_ref[...]).astype(BF16)
    tables = (cos_ref[...], sup_ref[...], sdn_ref[...])

    def project(h, w_col, out_ref, out_col, width, rope_cols, scale=None):
        res = jnp.dot(h, w_ref[:, w_col:w_col + width], preferred_element_type=F32)
        for c in range(0, rope_cols, LANES):
            blk = _rope_block(res[:, c:c + LANES], *tables)
            blk = blk if scale is None else blk * scale
            out_ref[:, out_col + c:out_col + c + LANES] = blk.astype(out_ref.dtype)
        if rope_cols < width:
            out_ref[:, out_col + rope_cols:out_col + width] = res[:, rope_cols:]

    for c0 in range(0, d, PROJ_COLS):
        project(hq, c0, q_ref, c0, PROJ_COLS, PROJ_COLS, scale=HEAD_DIM ** -0.5)
    project(hk, d, kv_ref, 0, 2 * k_width, k_width)


def _qkv(x, gq, gk, w, tables, *, tm, table_blocks, q_dtype):
    m, d = x.shape
    n_kv = w.shape[1] - d
    tspec = pl.BlockSpec((tm, LANES), lambda i: (i % table_blocks, 0))
    return pl.pallas_call(
        _qkv_kernel,
        grid=(m // tm,),
        in_specs=[
            pl.BlockSpec((tm, d), lambda i: (i, 0)),
            _resident((1, d)),
            _resident((1, d)),
            _resident(w.shape),
            tspec, tspec, tspec,
        ],
        out_specs=[
            pl.BlockSpec((tm, d), lambda i: (i, 0)),
            pl.BlockSpec((tm, n_kv), lambda i: (i, 0)),
        ],
        out_shape=[jax.ShapeDtypeStruct((m, d), q_dtype), jax.ShapeDtypeStruct((m, n_kv), F32)],
        compiler_params=_params("parallel"),
        name="qkv",
    )(x, gq, gk, w, *tables)


def _qk(q, k):
    return lax.dot_general(q, k, (((1,), (1,)), ((), ())), preferred_element_type=F32)


ATTN_Q_UNROLL = 4


def _attn_prompt_kernel(q_ref, kvp_ref, kvc_ref, x_ref, wo_ref, sink_ref, o_ref, kv_ref, ot_ref):
    blk = kvp_ref.shape[1]
    n_blk = q_ref.shape[1] // blk
    k_width = kvc_ref.shape[2] // 2
    group = q_ref.shape[2] // k_width
    n_kv_heads = k_width // HEAD_DIM
    kv_ref[0:blk, :] = kvp_ref[0].astype(BF16)
    kv_ref[blk:, :] = kvc_ref[0].astype(BF16)

    key = lax.broadcasted_iota(jnp.int32, (2 * blk, group * blk), 0)
    qry = lax.broadcasted_iota(jnp.int32, (2 * blk, group * blk), 1) % blk
    own = (key >= blk) & (key - blk <= qry)
    bias_inner = jnp.where(((key < blk) & (key > qry)) | own, 0.0, -jnp.inf).astype(F32)
    bias_first = jnp.where(pl.program_id(1) > 0, bias_inner, jnp.where(own, 0.0, -jnp.inf))
    head_of_lane = lax.broadcasted_iota(jnp.int32, (1, group * blk), 1) // blk

    def q_blocks(it, carry):
        def rows(ql):
            return pl.multiple_of((it * ATTN_Q_UNROLL + ql) * blk, blk)

        def scores(ql, j):
            row0 = rows(ql)
            bias = jnp.where(it * ATTN_Q_UNROLL + ql == 0, bias_first, bias_inner) if ql == 0 else bias_inner
            kk = kv_ref[pl.ds(row0, 2 * blk), j * HEAD_DIM:(j + 1) * HEAD_DIM]
            qs = jnp.concatenate(
                [q_ref[0, pl.ds(row0, blk), h * HEAD_DIM:(h + 1) * HEAD_DIM]
                 for h in range(j * group, (j + 1) * group)], axis=0)
            return _qk(kk, qs) + bias

        items = [(ql, j) for ql in range(ATTN_Q_UNROLL) for j in range(n_kv_heads)]
        st = scores(*items[0])
        for idx, (ql, j) in enumerate(items):
            st_next = scores(*items[idx + 1]) if idx + 1 < len(items) else None
            row0 = rows(ql)
            vv = kv_ref[pl.ds(row0, 2 * blk), k_width + j * HEAD_DIM:k_width + (j + 1) * HEAD_DIM]
            sink = jnp.zeros((1, group * blk), F32)
            for g in range(group):
                sink = jnp.where(head_of_lane == g, sink_ref[j * group + g], sink)
            m = jnp.maximum(jnp.max(st, axis=0, keepdims=True), sink)
            p = jnp.exp(st - m)
            denom = jnp.sum(p, axis=0, keepdims=True) + jnp.exp(sink - m)
            ot = lax.dot_general(vv, p.astype(BF16), (((0,), (0,)), ((), ())),
                                 preferred_element_type=F32) * (1.0 / denom)
            for g in range(group):
                h = j * group + g
                ot_ref[h * HEAD_DIM:(h + 1) * HEAD_DIM, pl.ds(row0, blk)] = (
                    ot[:, g * blk:(g + 1) * blk].astype(BF16))
            st = st_next
        return carry

    lax.fori_loop(0, n_blk // ATTN_Q_UNROLL, q_blocks, 0)

    proj = lax.dot_general(ot_ref[...], wo_ref[...], (((0,), (0,)), ((), ())),
                           preferred_element_type=F32)
    o_ref[0] = x_ref[0] + proj


def _attn_prompt(q, kv, x, wo, sinks, *, tq):
    b, t, d = x.shape
    n_kv = kv.shape[2]
    blk = WINDOW
    per = tq // blk
    assert per % ATTN_Q_UNROLL == 0
    return pl.pallas_call(
        _attn_prompt_kernel,
        grid=(b, t // tq),
        in_specs=[
            pl.BlockSpec((1, tq, d), lambda bi, n: (bi, n, 0)),
            pl.BlockSpec((1, blk, n_kv), lambda bi, n: (bi, jnp.maximum(n * per - 1, 0), 0)),
            pl.BlockSpec((1, tq, n_kv), lambda bi, n: (bi, n, 0)),
            pl.BlockSpec((1, tq, d), lambda bi, n: (bi, n, 0)),
            _resident((d, d)),
            pl.BlockSpec(memory_space=pltpu.SMEM),
        ],
        out_specs=pl.BlockSpec((1, tq, d), lambda bi, n: (bi, n, 0)),
        out_shape=jax.ShapeDtypeStruct((b, t, d), F32),
        scratch_shapes=[pltpu.VMEM((blk + tq, n_kv), BF16), pltpu.VMEM((d, tq), BF16)],
        compiler_params=_params("parallel", "arbitrary"),
        name="attn_prompt",
    )(q, kv, kv, x, wo, sinks)


def _attn_sample_kernel(q_ref, kvn_ref, ck_ref, cv_ref, x_ref, wo_ref, sink_ref,
                        o_ref, wk_ref, wv_ref, ocat_ref):
    gb, s, d = q_ref.shape
    _, n_kv_heads, hd, win = ck_ref.shape
    k_width = n_kv_heads * hd
    group = d // k_width
    step = pl.program_id(0)

    lane = lax.broadcasted_iota(jnp.int32, (1, win), 1)
    is_new = lane >= win - s
    qi = lax.broadcasted_iota(jnp.int32, (group * s, win), 0) % s
    kl = lax.broadcasted_iota(jnp.int32, (group * s, win), 1)
    bias_new = jnp.where(kl + s <= qi + win, 0.0, -jnp.inf).astype(F32)
    bias_old = jnp.where((kl < s) & (kl > qi), 0.0, -jnp.inf).astype(F32)
    head_of_row = lax.broadcasted_iota(jnp.int32, (group * s, 1), 0) // s
    nt = (((1,), (1,)), ((), ()))

    def one_batch(bb, carry):
        row0 = pl.multiple_of((step * gb + bb) * s, s)
        qb = q_ref[bb]
        kn = kvn_ref[bb]
        pad = jnp.zeros((win - s, k_width), F32)
        kn_t = jnp.concatenate([pad, kn[:, :k_width]], axis=0).T
        vn_t = jnp.concatenate([pad, kn[:, k_width:]], axis=0).T

        def window(old, new_t, j):
            shifted = pltpu.roll(old, win - s, 1)
            return jnp.where(is_new, new_t[j * hd:(j + 1) * hd, :], shifted)

        def scores(j):
            k_old = ck_ref[bb, j]
            k_win = window(k_old, kn_t, j)
            wk_ref[bb, j] = k_win
            qs = jnp.concatenate([qb[:, h * hd:(h + 1) * hd]
                                  for h in range(j * group, (j + 1) * group)], axis=0)
            qs = qs.astype(BF16)
            s_new = jnp.dot(qs, k_win.astype(BF16), preferred_element_type=F32) + bias_new
            s_old = jnp.dot(qs, k_old.astype(BF16), preferred_element_type=F32) + bias_old
            return s_new, s_old

        sc = scores(0)
        for j in range(n_kv_heads):
            sc_next = scores(j + 1) if j + 1 < n_kv_heads else None
            v_old = cv_ref[bb, j]
            v_win = window(v_old, vn_t, j)
            wv_ref[bb, j] = v_win
            sink = jnp.zeros((group * s, 1), F32)
            for g in range(group):
                sink = jnp.where(head_of_row == g, sink_ref[j * group + g], sink)
            s_new, s_old = sc
            m = jnp.maximum(jnp.maximum(jnp.max(s_new, axis=-1, keepdims=True),
                                        jnp.max(s_old, axis=-1, keepdims=True)), sink)
            p_new = jnp.exp(s_new - m)
            p_old = jnp.exp(s_old - m)
            denom = (jnp.sum(p_new, axis=-1, keepdims=True) + jnp.sum(p_old, axis=-1, keepdims=True)
                     + jnp.exp(sink - m))
            o = (lax.dot_general(p_new.astype(BF16), v_win.astype(BF16), nt,
                                 preferred_element_type=F32)
                 + lax.dot_general(p_old.astype(BF16), v_old.astype(BF16), nt,
                                   preferred_element_type=F32)) * (1.0 / denom)
            for g in range(group):
                h = j * group + g
                ocat_ref[pl.ds(row0, s), h * hd:(h + 1) * hd] = o[g * s:(g + 1) * s]
            sc = sc_next
        return carry

    lax.fori_loop(0, gb, one_batch, 0, unroll=2)

    @pl.when(step == pl.num_programs(0) - 1)
    def _():
        o_ref[...] = x_ref[...] + jnp.dot(ocat_ref[...].astype(BF16), wo_ref[...],
                                          preferred_element_type=F32)


def _attn_sample(q, kvn, ck, cv, x, wo, sinks, *, gb):
    nb, s, d = q.shape
    m = nb * s
    to_lanes, from_lanes = (0, 2, 3, 1), (0, 3, 1, 2)
    ck_t, cv_t = ck.transpose(to_lanes), cv.transpose(to_lanes)
    cache_spec = pl.BlockSpec((gb,) + ck_t.shape[1:], lambda i: (i, 0, 0, 0))
    x, wk_t, wv_t = pl.pallas_call(
        _attn_sample_kernel,
        grid=(nb // gb,),
        in_specs=[
            pl.BlockSpec((gb, s, d), lambda i: (i, 0, 0)),
            pl.BlockSpec((gb, s, kvn.shape[2]), lambda i: (i, 0, 0)),
            cache_spec,
            cache_spec,
            _resident((m, d)),
            _resident((d, d)),
            pl.BlockSpec(memory_space=pltpu.SMEM),
        ],
        out_specs=[pl.BlockSpec((m, d), lambda i: (0, 0)), cache_spec, cache_spec],
        out_shape=[
            jax.ShapeDtypeStruct((m, d), F32),
            jax.ShapeDtypeStruct(ck_t.shape, F32),
            jax.ShapeDtypeStruct(cv_t.shape, F32),
        ],
        scratch_shapes=[pltpu.VMEM((m, d), F32)],
        compiler_params=_params("arbitrary"),
        name="attn_sample",
    )(q, kvn, ck_t, cv_t, x, wo, sinks)
    return x, wk_t.transpose(from_lanes), wv_t.transpose(from_lanes)


def _row(v):
    return v.reshape(1, -1)


def _as_list(out):
    return list(out) if isinstance(out, (list, tuple)) else [out]


def _tile(m, want):
    return want if m % want == 0 else m


def kernel(x_prompt, x_sample, state_conv, cache_k, cache_v, norm_mix, w_pw1, b_pw1, w_dw, b_dw,
           conv_ln_g, conv_ln_b, w_pw2, b_pw2, kv_norm, w_k, w_v, w_q, w_o, sinks, norm_mlp,
           w_up, w_down, final_norm):
    b, t, d = x_prompt.shape
    nb, s, _ = x_sample.shape
    win = cache_k.shape[1]
    depth = norm_mlp.shape[0]
    n_conv = w_pw1.shape[0]
    assert depth == 2 and n_conv == 1 and w_q.shape[0] == 1, "one conv layer then one attention layer"
    assert t % WINDOW == 0 and win == WINDOW

    w_pw2b = w_pw2[0].astype(BF16)
    w_ob = w_o[0].astype(BF16)
    w_qkvb = jnp.concatenate([w_q[0], w_k, w_v], axis=1).astype(BF16)
    ln_pw2_w = (_row(conv_ln_g[0]), _row(conv_ln_b[0]), w_pw2b, _row(b_pw2[0]))

    def tokens(x, *, mix, attend, tables, table_blocks, tm_qkv, q_dtype, mlp_weights):
        m = x.shape[0]
        tm = _tile(m, MLP_ROWS)
        tf = MLP_HIDDEN_F32 if "layer" in mlp_weights(0) else MLP_HIDDEN
        fg = _row(final_norm)
        x, conv_state = mix(x)
        x, *w0 = _as_list(_mlp(x, _row(norm_mlp[0]), fg=fg, final=False, tm=tm, tf=tf,
                               **mlp_weights(0)))
        q, kv = _qkv(x, _row(norm_mix[1]), _row(kv_norm), w_qkvb, tables, tm=tm_qkv,
                     table_blocks=table_blocks, q_dtype=q_dtype)
        x, extra = attend(q, kv, x)
        y, *w1 = _as_list(_mlp(x, _row(norm_mlp[1]), fg=fg, final=True, tm=tm, tf=tf,
                               **mlp_weights(1)))
        return y, conv_state, kv, extra, (w0, w1)

    def mix_p(x):
        c, u_last = _glu_conv(x, _row(norm_mix[0]), *pw1_b, _row(b_pw1[0]), w_dw[0], _row(b_dw[0]),
                              seq_len=t, tm=ROW_TILE, tn=GLU_CONV_COLS)
        return _ln_pw2(c, x, *ln_pw2_w, tm=ROW_TILE), u_last[:, HALO - (CONV_WIDTH - 1):]

    def attend_p(q, kv, x):
        x3 = _attn_prompt(q.reshape(b, t, d), kv.reshape(b, t, -1), x.reshape(b, t, d), w_ob,
                          sinks[0], tq=ROW_TILE)
        return x3.reshape(b * t, d), None

    pos_s = PAST_LEN + jnp.arange(s, dtype=jnp.int32)

    pw1_b = []

    def mix_s(x):
        u, *pw1_halves = _glu(x, _row(norm_mix[0]), w_pw1[0], _row(b_pw1[0]), tn=GLU_COLS)
        pw1_b.extend(pw1_halves)
        full = jnp.concatenate([state_conv[0], u.reshape(nb, s, d)], axis=1)
        x = _conv_sample(full, x, w_dw[0], _row(b_dw[0]), *ln_pw2_w)
        return x, full[:, -(CONV_WIDTH - 1):]

    def attend_s(q, kv, x):
        x, wk, wv = _attn_sample(q.reshape(nb, s, d), kv.reshape(nb, s, -1), cache_k, cache_v,
                                 x, w_ob, sinks[0], gb=SAMPLE_BATCH_GROUP)
        return x, (wk, wv)

    tables_s = tuple(jnp.tile(tb, (nb, 1)) for tb in _rope_tables(pos_s))
    y_s, conv_s_state, _, (win_k_s, win_v_s), mlp_b = tokens(
        x_sample.reshape(nb * s, d), mix=mix_s, attend=attend_s, tables=tables_s,
        table_blocks=1, tm_qkv=nb * s, q_dtype=F32,
        mlp_weights=lambda l: dict(wu=w_up, wd=w_down, layer=l))

    tm_p = _tile(b * t, ROW_TILE)
    y_p, conv_p_state, kv_p, _, _ = tokens(
        x_prompt.reshape(b * t, d), mix=mix_p, attend=attend_p,
        tables=_rope_tables(jnp.arange(t, dtype=jnp.int32)), table_blocks=t // tm_p, tm_qkv=tm_p,
        q_dtype=BF16,
        mlp_weights=lambda l: dict(wu=mlp_b[l][0], wd=mlp_b[l][1]))
    kv_p = kv_p.reshape(b, t, -1)[:, t - WINDOW:].reshape(b, WINDOW, 2, N_KV_HEADS, HEAD_DIM)

    return (y_p.reshape(b, t, d), y_s.reshape(nb, s, d), conv_p_state[None], conv_s_state[None],
            kv_p[:, :, 0], kv_p[:, :, 1], win_k_s, win_v_s)
```
